```python
import math
import jax, jax.numpy as jnp
from jax import lax
import numpy as np

D_MODEL = 2048
BATCH = 8
SEQ = 2048
DEPTH = 2
DEC_BATCH = 128
DEC_SEQ = 4
PAST_LEN = 8192
PAGE_SIZE = 128

MIX_W = D_MODEL // 2
N_BRANCH = 3
LRU_BLOCKS = 8
LRU_BD = MIX_W // LRU_BLOCKS
CONV_W = 4
LRU_C = 8.0
HG_HEADS = 8
HG_DK = MIX_W // HG_HEADS
HG_DV = MIX_W // HG_HEADS
HG_CHUNK = 64
ATT_HEADS = 16
HEAD_DIM = MIX_W // ATT_HEADS
KV_HEADS = 4
GROUP = ATT_HEADS // KV_HEADS
WINDOW = 128
ATT_SCALE = HEAD_DIM ** -0.5
MASK_VALUE = -1e30
PEER_HEADS = 8
N_KEYS = 128
N_EXPERTS = N_KEYS * N_KEYS
PEER_TOPK = 16
PEER_DKEY = 128
PEER_BLOCK = 256
EPS = 1e-6
IN_WIDTHS = (MIX_W, MIX_W, MIX_W, MIX_W, MIX_W, ATT_HEADS * HEAD_DIM, KV_HEADS * HEAD_DIM, KV_HEADS * HEAD_DIM, N_BRANCH * D_MODEL)
IN_COLS = sum(IN_WIDTHS)

kernel_name = 'hybrid_rglru_hgrn2_swa_peer_step'


def rmsnorm(x, g):
    xf = x.astype(jnp.float32)
    y = xf * lax.rsqrt(jnp.mean(xf * xf, axis=-1, keepdims=True) + EPS)
    return (y * g.astype(jnp.float32)).astype(x.dtype)


def causal_conv(x, buf, w, b):
    T = x.shape[1]
    xp = jnp.concatenate([buf.astype(x.dtype), x], axis=1)
    y = b + xp[:, 0:T] * w[0]
    for j in range(1, CONV_W):
        y = y + xp[:, j:j + T] * w[j]
    return y, xp[:, -(CONV_W - 1):]


def _lin_combine(e1, e2):
    return (e1[0] * e2[0], e2[0] * e1[1] + e2[1])


def rglru(x, h0, wa, ba, wx, bx, lam):
    B, T, _ = x.shape
    xb = x.reshape(B, T, LRU_BLOCKS, LRU_BD)
    r = jax.nn.sigmoid(jnp.einsum('btni,nij->btnj', xb, wa).reshape(B, T, MIX_W) + ba)
    ig = jax.nn.sigmoid(jnp.einsum('btni,nij->btnj', xb, wx).reshape(B, T, MIX_W) + bx)
    log_a = -LRU_C * r * jax.nn.softplus(-lam)
    a = jnp.exp(log_a)
    u = jnp.sqrt(jnp.maximum(-jnp.expm1(2.0 * log_a), 0.0)) * (ig * x)
    a_cum, u_cum = lax.associative_scan(_lin_combine, (a, u), axis=1)
    h = a_cum * h0[:, None].astype(x.dtype) + u_cum
    return h, h[:, -1]


def hgrn_lower_bounds(p):
    sm = jax.nn.softmax(p.astype(jnp.float32), axis=0)
    return jnp.cumsum(sm, axis=0) - sm[0:1]


def gated_state_recurrence(q, k, log_f, v, s0):
    B, T, H, DK = q.shape
    C = math.gcd(T, HG_CHUNK)
    nc = T // C

    def to_chunks(t):
        return jnp.moveaxis(t.reshape(B, nc, C, *t.shape[2:]), 1, 0)

    causal = jnp.tril(jnp.ones((C, C), dtype=bool))[None, :, :, None, None]

    def step(S, inp):
        qc, kc, gc, vc = inp
        b = jnp.cumsum(gc, axis=1)
        diff = jnp.minimum(b[:, :, None] - b[:, None, :], 0.0)
        decay = jnp.where(causal, jnp.exp(diff), 0.0)
        scores = jnp.einsum('bthd,btshd,bshd->bhts', qc, decay, kc)
        o = (jnp.einsum('bthd,bhde->bthe', qc * jnp.exp(b), S)
             + jnp.einsum('bhts,bshe->bthe', scores, vc))
        b_last = b[:, -1]
        S = (jnp.exp(b_last)[..., None] * S
             + jnp.einsum('bshd,bshe->bhde', kc * jnp.exp(b_last[:, None] - b), vc))
        return S, o

    s_T, o = lax.scan(step, s0.astype(jnp.float32),
                      (to_chunks(q), to_chunks(k), to_chunks(log_f), to_chunks(v)))
    o = jnp.moveaxis(o, 0, 1).reshape(B, T, H, v.shape[-1])
    return o, s_T


def hgrn2(q, f, i, g, s0, lb, norm_g):
    B, T, _ = q.shape
    lb = lb.astype(jnp.float32)
    forget = lb + (1.0 - lb) * jax.nn.sigmoid(f.astype(jnp.float32))
    log_f = jnp.log(forget)
    key = 1.0 - forget
    qh = jax.nn.silu(q.astype(jnp.float32)).reshape(B, T, HG_HEADS, HG_DK)
    o, s_T = gated_state_recurrence(qh, key.reshape(B, T, HG_HEADS, HG_DK),
                                    log_f.reshape(B, T, HG_HEADS, HG_DK),
                                    i.astype(jnp.float32).reshape(B, T, HG_HEADS, HG_DV), s0)
    o = rmsnorm(o, norm_g.reshape(HG_HEADS, HG_DV)).reshape(B, T, MIX_W).astype(q.dtype)
    return o * jax.nn.silu(g), s_T.astype(s0.dtype)


def sink_attention(q, k, v, mask, sink):
    s = jnp.einsum('...qkgd,...skd->...kgqs', q, k).astype(jnp.float32) * ATT_SCALE
    s = jnp.where(mask, s, MASK_VALUE)
    sk = sink.astype(jnp.float32)[:, :, None, None]
    m = jnp.maximum(jnp.max(s, axis=-1, keepdims=True), sk)
    e = jnp.exp(s - m)
    p = e / (jnp.sum(e, axis=-1, keepdims=True) + jnp.exp(sk - m))
    return jnp.einsum('...kgqs,...skd->...qkgd', p.astype(v.dtype), v)


def swa(q, k, v, kv_buf, sink):
    B, T, _ = q.shape
    q = q.reshape(B, T, KV_HEADS, GROUP, HEAD_DIM)
    k = k.reshape(B, T, KV_HEADS, HEAD_DIM)
    v = v.reshape(B, T, KV_HEADS, HEAD_DIM)
    sink = sink.reshape(KV_HEADS, GROUP)
    if kv_buf is None:
        nb = T // WINDOW
        qb = q.reshape(B, nb, WINDOW, KV_HEADS, GROUP, HEAD_DIM)
        kb = k.reshape(B, nb, WINDOW, KV_HEADS, HEAD_DIM)
        vb = v.reshape(B, nb, WINDOW, KV_HEADS, HEAD_DIM)
        kk = jnp.concatenate([jnp.concatenate([jnp.zeros_like(kb[:, :1]), kb[:, :-1]], axis=1), kb], axis=2)
        vv = jnp.concatenate([jnp.concatenate([jnp.zeros_like(vb[:, :1]), vb[:, :-1]], axis=1), vb], axis=2)
        i = jnp.arange(WINDOW)[:, None]
        j = jnp.arange(2 * WINDOW)[None, :]
        band = (j <= i + WINDOW) & (j > i)
        valid = (jnp.arange(nb)[:, None, None] * WINDOW - WINDOW + j[None]) >= 0
        mask = (band[None] & valid)[:, None, None]
        o = sink_attention(qb, kk, vv, mask, sink).reshape(B, T, MIX_W)
        return o, k[:, -WINDOW:], v[:, -WINDOW:]
    kbuf, vbuf = kv_buf
    wb = kbuf.shape[1]
    kk = jnp.concatenate([kbuf.astype(k.dtype), k], axis=1)
    vv = jnp.concatenate([vbuf.astype(v.dtype), v], axis=1)
    i = jnp.arange(T)[:, None]
    j = jnp.arange(wb + T)[None, :]
    mask = (j <= wb + i) & (j > wb + i - WINDOW)
    o = sink_attention(q, kk, vv, mask, sink).reshape(B, T, MIX_W)
    return o, kk[:, -wb:], vv[:, -wb:]


def peer(x, wq, sub_keys, u, v):
    B, T, D = x.shape
    n = B * T
    blk = min(PEER_BLOCK, n)
    nblk = -(-n // blk)
    xt = jnp.pad(x.reshape(n, D), ((0, nblk * blk - n), (0, 0))).reshape(nblk, blk, D)

    def block(xb):
        q = (xb @ wq).reshape(blk, PEER_HEADS, 2, PEER_DKEY)
        s = jnp.einsum('thcd,hcnd->thcn', q, sub_keys).astype(jnp.float32)
        top_s, top_i = lax.top_k(s, PEER_TOPK)
        cand_s = (top_s[:, :, 0, :, None] + top_s[:, :, 1, None, :]).reshape(blk, PEER_HEADS, PEER_TOPK * PEER_TOPK)
        cand_i = (top_i[:, :, 0, :, None] * N_KEYS + top_i[:, :, 1, None, :]).reshape(blk, PEER_HEADS, PEER_TOPK * PEER_TOPK)
        sel_s, sel = lax.top_k(cand_s, PEER_TOPK)
        idx = jnp.take_along_axis(cand_i, sel, axis=-1)
        gate = jax.nn.softmax(sel_s, axis=-1).astype(xb.dtype)
        ue = jnp.take(u, idx, axis=0)
        act = jax.nn.gelu(jnp.einsum('td,thkd->thk', xb, ue), approximate=False)
        ve = jnp.take(v, idx, axis=0)
        return jnp.einsum('thk,thkd->td', gate * act, ve)

    y = lax.map(block, xt).reshape(nblk * blk, D)[:n]
    return y.reshape(B, T, D)


def decoder_layer(x, conv0, h0, s0, kv_buf, lw):
    (g1, w_in, conv_w, conv_b, wa, ba, wx, bx, lam, lb, hg_g, sink,
     w_br, w_out, g2, wq, pkeys, pu, pv) = lw
    B, T, _ = x.shape
    xn = rmsnorm(x, g1)
    proj = xn @ w_in
    offsets = np.cumsum(IN_WIDTHS)[:-1].tolist()
    xa, qb, fb, ib, gb, qc, kc, vc, gates = jnp.split(proj, offsets, axis=-1)
    xa, conv_new = causal_conv(xa, conv0, conv_w, conv_b)
    ya, h_new = rglru(xa, h0, wa, ba, wx, bx, lam)
    yb, s_new = hgrn2(qb, fb, ib, gb, s0, lb, hg_g)
    yc, k_new, v_new = swa(qc, kc, vc, kv_buf, sink)
    branches = jnp.stack([ya, yb.astype(ya.dtype), yc], axis=2)
    proj_b = jnp.einsum('btnm,nmd->btnd', branches, w_br)
    g = jax.nn.sigmoid(gates.reshape(B, T, N_BRANCH, D_MODEL))
    x = x + jnp.sum(g * proj_b, axis=2) @ w_out
    x = x + peer(rmsnorm(x, g2), wq, pkeys, pu, pv)
    return x, (conv_new, h_new, s_new, k_new, v_new)


def setup_inputs(seed: int = 0) -> dict:
    key = jax.random.key(seed)
    ks = jax.random.split(key, 32)
    nrm = jax.random.normal
    wb = min(WINDOW, PAST_LEN)
    a0 = jax.random.uniform(ks[12], (DEPTH, MIX_W), minval=0.9, maxval=0.999)
    a1 = a0 ** (1.0 / LRU_C)
    return {
        'x_prompt': nrm(ks[0], (BATCH, SEQ, D_MODEL), jnp.float32),
        'x_sample': nrm(ks[1], (DEC_BATCH, DEC_SEQ, D_MODEL), jnp.float32),
        'state_conv': nrm(ks[2], (DEPTH, DEC_BATCH, CONV_W - 1, MIX_W), jnp.float32),
        'state_lru': 0.5 * nrm(ks[3], (DEPTH, DEC_BATCH, MIX_W), jnp.float32),
        'state_hgrn': 0.5 * nrm(ks[4], (DEPTH, DEC_BATCH, HG_HEADS, HG_DK, HG_DV), jnp.float32),
        'cache_k': nrm(ks[5], (DEPTH, DEC_BATCH, wb, KV_HEADS, HEAD_DIM), jnp.float32),
        'cache_v': nrm(ks[6], (DEPTH, DEC_BATCH, wb, KV_HEADS, HEAD_DIM), jnp.float32),
        'norm1_g': 1.0 + 0.02 * nrm(ks[7], (DEPTH, D_MODEL), jnp.float32),
        'w_in': nrm(ks[8], (DEPTH, D_MODEL, IN_COLS), jnp.float32) * D_MODEL ** -0.5,
        'conv_w': nrm(ks[9], (DEPTH, CONV_W, MIX_W), jnp.float32) * CONV_W ** -0.5,
        'conv_b': 0.01 * nrm(ks[10], (DEPTH, MIX_W), jnp.float32),
        'lru_wa': nrm(ks[11], (DEPTH, LRU_BLOCKS, LRU_BD, LRU_BD), jnp.float32) * LRU_BD ** -0.5,
        'lru_ba': 0.01 * nrm(ks[13], (DEPTH, MIX_W), jnp.float32),
        'lru_wx': nrm(ks[14], (DEPTH, LRU_BLOCKS, LRU_BD, LRU_BD), jnp.float32) * LRU_BD ** -0.5,
        'lru_bx': 0.01 * nrm(ks[15], (DEPTH, MIX_W), jnp.float32),
        'lru_lambda': jnp.log(a1) - jnp.log1p(-a1),
        'hg_lb': 0.5 * nrm(ks[16], (DEPTH, MIX_W), jnp.float32),
        'hg_norm_g': 1.0 + 0.02 * nrm(ks[17], (DEPTH, MIX_W), jnp.float32),
        'attn_sink': 0.5 * nrm(ks[18], (DEPTH, ATT_HEADS), jnp.float32),
        'w_branch': nrm(ks[19], (DEPTH, N_BRANCH, MIX_W, D_MODEL), jnp.float32) * MIX_W ** -0.5,
        'w_out': nrm(ks[20], (DEPTH, D_MODEL, D_MODEL), jnp.float32) * D_MODEL ** -0.5,
        'norm2_g': 1.0 + 0.02 * nrm(ks[21], (DEPTH, D_MODEL), jnp.float32),
        'peer_wq': nrm(ks[22], (DEPTH, D_MODEL, PEER_HEADS * 2 * PEER_DKEY), jnp.float32) * D_MODEL ** -0.5,
        'peer_keys': nrm(ks[23], (DEPTH, PEER_HEADS, 2, N_KEYS, PEER_DKEY), jnp.float32) * PEER_DKEY ** -0.5,
        'peer_u': nrm(ks[24], (DEPTH, N_EXPERTS, D_MODEL), jnp.float32) * D_MODEL ** -0.5,
        'peer_v': nrm(ks[25], (DEPTH, N_EXPERTS, D_MODEL), jnp.float32) * PEER_HEADS ** -0.5,
        'final_g': 1.0 + 0.02 * nrm(ks[26], (D_MODEL,), jnp.float32),
    }


def reference(x_prompt, x_sample, state_conv, state_lru, state_hgrn, cache_k, cache_v,
              norm1_g, w_in, conv_w, conv_b, lru_wa, lru_ba, lru_wx, lru_bx, lru_lambda,
              hg_lb, hg_norm_g, attn_sink, w_branch, w_out, norm2_g,
              peer_wq, peer_keys, peer_u, peer_v, final_g):
    lower = hgrn_lower_bounds(hg_lb)
    xp, xs = x_prompt, x_sample
    bp = xp.shape[0]
    p_st, s_st = [], []
    for l in range(DEPTH):
        lw = (norm1_g[l], w_in[l], conv_w[l], conv_b[l], lru_wa[l], lru_ba[l], lru_wx[l], lru_bx[l],
              lru_lambda[l], lower[l], hg_norm_g[l], attn_sink[l], w_branch[l], w_out[l], norm2_g[l],
              peer_wq[l], peer_keys[l], peer_u[l], peer_v[l])
        xp, st = decoder_layer(xp,
                               jnp.zeros((bp, CONV_W - 1, MIX_W), xp.dtype),
                               jnp.zeros((bp, MIX_W), xp.dtype),
                               jnp.zeros((bp, HG_HEADS, HG_DK, HG_DV), xp.dtype),
                               None, lw)
        p_st.append(st)
        xs, st = decoder_layer(xs, state_conv[l], state_lru[l], state_hgrn[l],
                               (cache_k[l], cache_v[l]), lw)
        s_st.append(st)
    y_prompt = rmsnorm(xp, final_g)
    y_sample = rmsnorm(xs, final_g)
    p_conv = jnp.stack([s[0] for s in p_st])
    p_lru = jnp.stack([s[1] for s in p_st])
    p_hgrn = jnp.stack([s[2] for s in p_st])
    p_k = jnp.stack([s[3] for s in p_st])
    p_v = jnp.stack([s[4] for s in p_st])
    s_conv = jnp.stack([s[0] for s in s_st])
    s_lru = jnp.stack([s[1] for s in s_st])
    s_hgrn = jnp.stack([s[2] for s in s_st])
    s_k = jnp.stack([s[3] for s in s_st])
    s_v = jnp.stack([s[4] for s in s_st])
    return (y_prompt, y_sample, p_conv, p_lru, p_hgrn, p_k, p_v, s_conv, s_lru, s_hgrn, s_k, s_v)
```

```python
import functools

import jax
import jax.numpy as jnp
from jax import lax
from jax.experimental import pallas as pl
from jax.experimental.pallas import tpu as pltpu

F32 = jnp.float32
BF16 = jnp.bfloat16

D_MODEL = 2048
MIX_W = D_MODEL // 2
N_BRANCH = 3
LRU_BLOCKS = 8
LRU_BD = MIX_W // LRU_BLOCKS
CONV_W = 4
LRU_C = 8.0
HG_HEADS = 8
HG_DK = MIX_W // HG_HEADS
ATT_HEADS = 16
HEAD_DIM = MIX_W // ATT_HEADS
KV_HEADS = 4
GROUP = ATT_HEADS // KV_HEADS
KV_W = KV_HEADS * HEAD_DIM
WINDOW = 128
ATT_SCALE = HEAD_DIM ** -0.5
MASK_VALUE = -1e30
PEER_HEADS = 8
N_KEYS = 128
PEER_TOPK = 16
PEER_DKEY = 128
PEER_SLOTS = PEER_HEADS * PEER_TOPK
EPS = 1e-6

COL_XA = 0
COL_HQ = MIX_W
COL_HF = 2 * MIX_W
COL_HI = 3 * MIX_W
COL_HG = 4 * MIX_W
COL_AQ = 5 * MIX_W
COL_AK = 6 * MIX_W
COL_AV = COL_AK + KV_W
COL_GATE = COL_AV + KV_W
IN_COLS = COL_GATE + N_BRANCH * D_MODEL

SUBLANES = 8
SAMPLE_PAD_T = SUBLANES
VMEM_LIMIT = 48 * 1024 * 1024


def _pick(n, candidates):
    for c in candidates:
        if n % c == 0:
            return c
    raise ValueError(f"no tile in {candidates} divides {n}")


def _cparams(sem, vmem=VMEM_LIMIT):
    return pltpu.CompilerParams(dimension_semantics=sem, vmem_limit_bytes=vmem)


def _rms(x, g):
    return x * lax.rsqrt(jnp.mean(x * x, axis=-1, keepdims=True) + EPS) * g


def _dot_nt(a, b):
    return lax.dot_general(a, b, (((1,), (1,)), ((), ())), preferred_element_type=F32)


def _dot_tn(a, b):
    return lax.dot_general(a, b, (((0,), (0,)), ((), ())), preferred_element_type=F32)


def _norm_matmul_kernel(x_ref, g_ref, w_ref, o_ref, xn_ref):
    @pl.when(pl.program_id(1) == 0)
    def _():
        xn_ref[...] = _rms(x_ref[...], g_ref[...]).astype(BF16)

    o_ref[...] = jnp.dot(xn_ref[...], w_ref[...], preferred_element_type=F32)


def _norm_matmul(x, g, w):
    n, k = x.shape
    c = w.shape[1]
    tm = _pick(n, (1024, 512, 256, 128, 64))
    tn = _pick(c, (512, 256, 128))
    return pl.pallas_call(
        _norm_matmul_kernel,
        out_shape=jax.ShapeDtypeStruct((n, c), F32),
        grid=(n // tm, c // tn),
        in_specs=[pl.BlockSpec((tm, k), lambda i, j: (i, 0)),
                  pl.BlockSpec((1, k), lambda i, j: (0, 0)),
                  pl.BlockSpec((k, tn), lambda i, j: (0, j))],
        out_specs=pl.BlockSpec((tm, tn), lambda i, j: (i, j)),
        scratch_shapes=[pltpu.VMEM((tm, k), BF16)],
        compiler_params=_cparams(("parallel", "arbitrary")),
        name="norm_in_proj",
    )(x, g.reshape(1, k), w)


def _lru_kernel(xa_ref, c0_ref, h0_ref, cw_ref, cb_ref, wa_ref, ba_ref, wx_ref, bx_ref, lam_ref,
                ya_ref, hl_ref, tail_ref, h_ref, *, tc, last_row):
    j = pl.program_id(1)

    @pl.when(j == 0)
    def _():
        tail_ref[...] = c0_ref[0]
        h_ref[...] = h0_ref[0]

    x = xa_ref[...]
    xe = jnp.concatenate([tail_ref[...], x], axis=0)
    y = cb_ref[...] + x * cw_ref[CONV_W - 1:CONV_W, :]
    for d in range(1, CONV_W):
        xs = pltpu.roll(xe, d, axis=0)[SUBLANES:SUBLANES + tc]
        y = y + xs * cw_ref[CONV_W - 1 - d:CONV_W - d, :]
    tail_ref[...] = x[tc - SUBLANES:tc]

    y16 = y.astype(BF16)
    ra, rx = [], []
    for n in range(LRU_BLOCKS):
        yb = y16[:, n * LRU_BD:(n + 1) * LRU_BD]
        ra.append(jnp.dot(yb, wa_ref[n], preferred_element_type=F32))
        rx.append(jnp.dot(yb, wx_ref[n], preferred_element_type=F32))
    r = jax.nn.sigmoid(jnp.concatenate(ra, axis=1) + ba_ref[...])
    ig = jax.nn.sigmoid(jnp.concatenate(rx, axis=1) + bx_ref[...])
    nl = -lam_ref[...]
    softplus = jnp.maximum(nl, 0.0) + jnp.log(1.0 + jnp.exp(-jnp.abs(nl)))
    log_a = -LRU_C * r * softplus
    a = jnp.exp(log_a)
    u = jnp.sqrt(jnp.maximum(1.0 - a * a, 0.0)) * (ig * y)

    rows = lax.broadcasted_iota(jnp.int32, (tc, MIX_W), 0)
    d = 1
    while d < tc:
        keep = rows >= d
        a_sh = jnp.where(keep, pltpu.roll(a, d, axis=0), 1.0)
        u_sh = jnp.where(keep, pltpu.roll(u, d, axis=0), 0.0)
        u = a * u_sh + u
        a = a * a_sh
        d *= 2
    h = a * h_ref[...] + u
    ya_ref[...] = h
    h_ref[...] = h[tc - 1:tc]

    @pl.when(j == pl.num_programs(1) - 1)
    def _():
        hl_ref[0] = h[last_row:last_row + 1]


def _conv_lru(proj, row0, nseq, t, t_valid, conv0, h0, lw):
    tc = _pick(t, (256, 128, 64, 32, 16, 8))
    nchunk = t // tc
    blk0 = row0 // tc
    last_row = (t_valid - 1) - (nchunk - 1) * tc
    const2 = lambda b, j: (0, 0)
    const3 = lambda b, j: (0, 0, 0)
    kern = functools.partial(_lru_kernel, tc=tc, last_row=last_row)
    ya, hl = pl.pallas_call(
        kern,
        out_shape=(jax.ShapeDtypeStruct((nseq * t, MIX_W), F32),
                   jax.ShapeDtypeStruct((nseq, 1, MIX_W), F32)),
        grid=(nseq, nchunk),
        in_specs=[pl.BlockSpec((tc, MIX_W), lambda b, j: (blk0 + b * nchunk + j, COL_XA // MIX_W)),
                  pl.BlockSpec((1, SUBLANES, MIX_W), lambda b, j: (b, 0, 0)),
                  pl.BlockSpec((1, 1, MIX_W), lambda b, j: (b, 0, 0)),
                  pl.BlockSpec((CONV_W, MIX_W), const2),
                  pl.BlockSpec((1, MIX_W), const2),
                  pl.BlockSpec((LRU_BLOCKS, LRU_BD, LRU_BD), const3),
                  pl.BlockSpec((1, MIX_W), const2),
                  pl.BlockSpec((LRU_BLOCKS, LRU_BD, LRU_BD), const3),
                  pl.BlockSpec((1, MIX_W), const2),
                  pl.BlockSpec((1, MIX_W), const2)],
        out_specs=(pl.BlockSpec((tc, MIX_W), lambda b, j: (b * nchunk + j, 0)),
                   pl.BlockSpec((1, 1, MIX_W), lambda b, j: (b, 0, 0))),
        scratch_shapes=[pltpu.VMEM((SUBLANES, MIX_W), F32), pltpu.VMEM((1, MIX_W), F32)],
        compiler_params=_cparams(("parallel", "arbitrary")),
        name="conv_rglru",
    )(proj, conv0, h0.reshape(nseq, 1, MIX_W), lw["conv_w"], lw["conv_b"], lw["wa"], lw["ba"],
      lw["wx"], lw["bx"], lw["lam"])
    return ya, hl.reshape(nseq, MIX_W)


def _hgrn_kernel(q_ref, f_ref, i_ref, g_ref, lb_ref, ng_ref, s0_ref, y_ref, st_out_ref, st_ref,
                 *, cc, t_valid, layer):
    c = pl.program_id(1)
    last = pl.num_programs(1) - 1

    @pl.when(c == 0)
    def _():
        for h in range(HG_HEADS):
            st_ref[h] = s0_ref[0, h].T

    lbp = lb_ref[...]
    e = jnp.exp(lbp - jnp.max(lbp, axis=0, keepdims=True))
    sm = e / jnp.sum(e, axis=0, keepdims=True)
    lower = jnp.zeros((1, MIX_W), F32)
    for i in range(1, layer + 1):
        lower = lower + sm[i:i + 1]

    rows = lax.broadcasted_iota(jnp.int32, (cc, HG_DK), 0)
    ti = lax.broadcasted_iota(jnp.int32, (cc, cc), 0)
    si = lax.broadcasted_iota(jnp.int32, (cc, cc), 1)
    tx = jnp.bitwise_xor(ti, si)
    causal = ti > si

    for h in range(HG_HEADS):
        sl = slice(h * HG_DK, (h + 1) * HG_DK)
        lb = lower[:, sl]
        forget = lb + (1.0 - lb) * jax.nn.sigmoid(f_ref[:, sl])
        g = jnp.log(forget)
        kk = 1.0 - forget
        if t_valid < cc:
            g = jnp.where(rows < t_valid, g, 0.0)
            kk = jnp.where(rows < t_valid, kk, 0.0)
        qh = jax.nn.silu(q_ref[:, sl])
        v16 = i_ref[:, sl].astype(BF16)

        b = g
        d = 1
        while d < cc:
            b = b + jnp.where(rows >= d, pltpu.roll(b, d, axis=0), 0.0)
            d *= 2

        scores = jnp.zeros((cc, cc), F32)
        fm = b
        m = 1
        while m < cc:
            odd = jnp.bitwise_and(rows, m) != 0
            em = jnp.where(rows >= m, pltpu.roll(fm, m, axis=0), 0.0)
            ez = jnp.exp(jnp.where(odd, b - em, fm - b))
            sc = _dot_nt((qh * ez).astype(BF16), (kk * ez).astype(BF16))
            scores = jnp.where((tx >= m) & (tx < 2 * m) & causal, sc, scores)
            fm = jnp.where(odd, fm, pltpu.roll(fm, cc - m, axis=0))
            m *= 2
        scores = jnp.where(ti == si, jnp.sum(qh * kk, axis=1, keepdims=True), scores)

        st = st_ref[h]
        b_last = b[cc - 1:cc]
        o = (_dot_nt((qh * jnp.exp(b)).astype(BF16), st.astype(BF16))
             + jnp.dot(scores.astype(BF16), v16, preferred_element_type=F32))
        st_new = st * jnp.exp(b_last) + _dot_tn(v16, (kk * jnp.exp(b_last - b)).astype(BF16))
        st_ref[h] = st_new

        on = _rms(o, ng_ref[:, sl])
        y_ref[:, sl] = on * jax.nn.silu(g_ref[:, sl])

        @pl.when(c == last)
        def _():
            st_out_ref[0, h] = st_new.T


def _hgrn(proj, row0, nseq, t, t_valid, s0, hg_lb, norm_g, layer):
    cc = _pick(t, (128, 64, 32, 16, 8))
    nchunk = t // cc
    blk0 = row0 // cc
    depth = hg_lb.shape[0]

    def col(c0):
        return pl.BlockSpec((cc, MIX_W), lambda b, j: (blk0 + b * nchunk + j, c0 // MIX_W))

    kern = functools.partial(_hgrn_kernel, cc=cc, t_valid=min(t_valid, cc) if nchunk == 1 else cc,
                             layer=layer)
    yb, st = pl.pallas_call(
        kern,
        out_shape=(jax.ShapeDtypeStruct((nseq * t, MIX_W), F32),
                   jax.ShapeDtypeStruct(s0.shape, F32)),
        grid=(nseq, nchunk),
        in_specs=[col(COL_HQ), col(COL_HF), col(COL_HI), col(COL_HG),
                  pl.BlockSpec((depth, MIX_W), lambda b, j: (0, 0)),
                  pl.BlockSpec((1, MIX_W), lambda b, j: (0, 0)),
                  pl.BlockSpec((1, HG_HEADS, HG_DK, HG_DK), lambda b, j: (b, 0, 0, 0))],
        out_specs=(pl.BlockSpec((cc, MIX_W), lambda b, j: (b * nchunk + j, 0)),
                   pl.BlockSpec((1, HG_HEADS, HG_DK, HG_DK), lambda b, j: (b, 0, 0, 0))),
        scratch_shapes=[pltpu.VMEM((HG_HEADS, HG_DK, HG_DK), F32)],
        compiler_params=_cparams(("parallel", "arbitrary")),
        name="hgrn2",
    )(proj, proj, proj, proj, hg_lb, norm_g.reshape(1, MIX_W), s0)
    return yb, st


def _swa_kernel(sink_ref, q_ref, kc_ref, vc_ref, kp_ref, vp_ref, o_ref, *, tq, prev_always):
    i = pl.program_id(1)
    rows = GROUP * tq
    ri = lax.broadcasted_iota(jnp.int32, (rows, WINDOW), 0)
    ci = lax.broadcasted_iota(jnp.int32, (rows, WINDOW), 1)
    grp = ri // tq
    qi = ri - grp * tq
    if prev_always:
        prev_ok = ci > qi
    else:
        prev_ok = ci > qi + jnp.where(i > 0, 0, WINDOW)
    cur_ok = ci <= qi
    grp1 = grp[:, 0:1]

    q = q_ref[...]
    outs = [None] * ATT_HEADS
    for kh in range(KV_HEADS):
        ks = slice(kh * HEAD_DIM, (kh + 1) * HEAD_DIM)
        kp = kp_ref[:, ks].astype(BF16)
        vp = vp_ref[:, ks].astype(BF16)
        kc = kc_ref[:, ks]
        vc = vc_ref[:, ks]
        if tq < WINDOW:
            pad = jnp.zeros((WINDOW - tq, HEAD_DIM), F32)
            kc = jnp.concatenate([kc, pad], axis=0)
            vc = jnp.concatenate([vc, pad], axis=0)
        kc = kc.astype(BF16)
        vc = vc.astype(BF16)
        qs = jnp.concatenate(
            [q[:, (kh * GROUP + g) * HEAD_DIM:(kh * GROUP + g + 1) * HEAD_DIM] for g in range(GROUP)],
            axis=0).astype(BF16)
        sp = jnp.where(prev_ok, _dot_nt(qs, kp) * ATT_SCALE, MASK_VALUE)
        sc = jnp.where(cur_ok, _dot_nt(qs, kc) * ATT_SCALE, MASK_VALUE)
        sk = jnp.zeros((rows, 1), F32)
        for g in range(GROUP):
            sk = jnp.where(grp1 == g, sink_ref[kh * GROUP + g], sk)
        mx = jnp.maximum(jnp.maximum(jnp.max(sp, axis=-1, keepdims=True),
                                     jnp.max(sc, axis=-1, keepdims=True)), sk)
        ep = jnp.exp(sp - mx)
        ec = jnp.exp(sc - mx)
        den = (jnp.sum(ep, axis=-1, keepdims=True) + jnp.sum(ec, axis=-1, keepdims=True)
               + jnp.exp(sk - mx))
        o = (jnp.dot((ep / den).astype(BF16), vp, preferred_element_type=F32)
             + jnp.dot((ec / den).astype(BF16), vc, preferred_element_type=F32))
        for g in range(GROUP):
            outs[kh * GROUP + g] = o[g * tq:(g + 1) * tq]
    o_ref[...] = jnp.concatenate(outs, axis=1)


def _swa(proj, row0, nseq, t, sink, cache=None):
    tq = _pick(t, (WINDOW, SUBLANES))
    nb = t // tq
    blk0 = row0 // tq
    kcol = COL_AK // KV_W
    vcol = COL_AV // KV_W
    cur = lambda c0: pl.BlockSpec((tq, KV_W), lambda b, j: (blk0 + b * nb + j, c0))
    if cache is None:
        kprev, vprev = proj, proj
        prev = lambda c0: pl.BlockSpec(
            (WINDOW, KV_W), lambda b, j: (blk0 + b * nb + jnp.maximum(j - 1, 0), c0))
        prev_specs = [prev(kcol), prev(vcol)]
    else:
        assert nb == 1
        kprev, vprev = cache
        prev_specs = [pl.BlockSpec((WINDOW, KV_W), lambda b, j: (b, 0))] * 2
    kern = functools.partial(_swa_kernel, tq=tq, prev_always=cache is not None)
    return pl.pallas_call(
        kern,
        out_shape=jax.ShapeDtypeStruct((nseq * t, MIX_W), F32),
        grid=(nseq, nb),
        in_specs=[pl.BlockSpec(memory_space=pltpu.SMEM),
                  pl.BlockSpec((tq, MIX_W), lambda b, j: (blk0 + b * nb + j, COL_AQ // MIX_W)),
                  cur(kcol), cur(vcol)] + prev_specs,
        out_specs=pl.BlockSpec((tq, MIX_W), lambda b, j: (b * nb + j, 0)),
        compiler_params=_cparams(("parallel", "arbitrary")),
        name="swa",
    )(sink, proj, proj, proj, kprev, vprev)


def _mix_kernel(ya_ref, yb_ref, yc_ref, g0_ref, g1_ref, g2_ref, w_ref, z_ref):
    acc = None
    for n, (y_ref, g_ref) in enumerate(((ya_ref, g0_ref), (yb_ref, g1_ref), (yc_ref, g2_ref))):
        p = jnp.dot(y_ref[...].astype(BF16), w_ref[n], preferred_element_type=F32)
        p = jax.nn.sigmoid(g_ref[...]) * p
        acc = p if acc is None else acc + p
    z_ref[...] = acc.astype(BF16)


def _branch_mix(ya, yb, yc, proj, w_br):
    n = ya.shape[0]
    tm = _pick(n, (512, 256, 128, 64))
    tn = 512
    ysp = pl.BlockSpec((tm, MIX_W), lambda i, j: (i, 0))
    gsp = lambda b: pl.BlockSpec((tm, tn), lambda i, j: (i, (COL_GATE + b * D_MODEL) // tn + j))
    return pl.pallas_call(
        _mix_kernel,
        out_shape=jax.ShapeDtypeStruct((n, D_MODEL), BF16),
        grid=(n // tm, D_MODEL // tn),
        in_specs=[ysp, ysp, ysp, gsp(0), gsp(1), gsp(2),
                  pl.BlockSpec((N_BRANCH, MIX_W, tn), lambda i, j: (0, 0, j))],
        out_specs=pl.BlockSpec((tm, tn), lambda i, j: (i, j)),
        compiler_params=_cparams(("parallel", "arbitrary")),
        name="branch_mix",
    )(ya, yb, yc, proj, proj, proj, w_br)


def _out_kernel(z_ref, w_ref, x_ref, o_ref):
    o_ref[...] = x_ref[...] + jnp.dot(z_ref[...], w_ref[...], preferred_element_type=F32)


def _out_proj(z, w, x):
    n = z.shape[0]
    tm = _pick(n, (1024, 512, 256, 128, 64))
    tn = 512
    return pl.pallas_call(
        _out_kernel,
        out_shape=jax.ShapeDtypeStruct((n, D_MODEL), F32),
        grid=(n // tm, D_MODEL // tn),
        in_specs=[pl.BlockSpec((tm, D_MODEL), lambda i, j: (i, 0)),
                  pl.BlockSpec((D_MODEL, tn), lambda i, j: (0, j)),
                  pl.BlockSpec((tm, tn), lambda i, j: (i, j))],
        out_specs=pl.BlockSpec((tm, tn), lambda i, j: (i, j)),
        compiler_params=_cparams(("parallel", "arbitrary")),
        name="out_proj",
    )(z, w, x)


def _peer_score_kernel(x_ref, g_ref, wq_ref, keys_ref, st_ref):
    xn = _rms(x_ref[...], g_ref[...]).astype(BF16)
    q16 = jnp.dot(xn, wq_ref[...], preferred_element_type=F32).astype(BF16)
    for hc in range(2 * PEER_HEADS):
        sl = slice(hc * PEER_DKEY, (hc + 1) * PEER_DKEY)
        st_ref[hc * N_KEYS:(hc + 1) * N_KEYS, :] = _dot_nt(keys_ref[hc], q16[:, sl])


def _peer_scores(x, g, wq, keys):
    n = x.shape[0]
    tm = _pick(n, (256, 128))
    rows = 2 * PEER_HEADS * N_KEYS
    return pl.pallas_call(
        _peer_score_kernel,
        out_shape=jax.ShapeDtypeStruct((rows, n), F32),
        grid=(n // tm,),
        in_specs=[pl.BlockSpec((tm, D_MODEL), lambda i: (i, 0)),
                  pl.BlockSpec((1, D_MODEL), lambda i: (0, 0)),
                  pl.BlockSpec((D_MODEL, rows), lambda i: (0, 0)),
                  pl.BlockSpec((2 * PEER_HEADS, N_KEYS, PEER_DKEY), lambda i: (0, 0, 0))],
        out_specs=pl.BlockSpec((rows, tm), lambda i: (0, i)),
        compiler_params=_cparams(("parallel",)),
        name="peer_scores",
    )(x, g.reshape(1, D_MODEL), wq, keys)


def _top_rows(s, idx_payload, k):
    r, l = s.shape
    riota = lax.broadcasted_iota(jnp.int32, (r, l), 0)
    kiota = lax.broadcasted_iota(jnp.int32, (k, l), 0)
    vals = jnp.zeros((k, l), F32)
    pay = jnp.zeros((k, l), jnp.int32)
    for j in range(k):
        m = jnp.max(s, axis=0, keepdims=True)
        pos = jnp.min(jnp.where(s == m, riota, r), axis=0, keepdims=True)
        hit = riota == pos
        if idx_payload is None:
            p = pos
        else:
            p = jnp.sum(jnp.where(hit, idx_payload, 0), axis=0, keepdims=True)
        s = jnp.where(hit, -jnp.inf, s)
        vals = jnp.where(kiota == j, m, vals)
        pay = jnp.where(kiota == j, p, pay)
    return vals, pay


def _peer_topk_kernel(st_ref, idx_ref, gate_ref):
    for h in range(PEER_HEADS):
        base = h * 2 * N_KEYS
        v0, i0 = _top_rows(st_ref[base:base + N_KEYS, :], None, PEER_TOPK)
        v1, i1 = _top_rows(st_ref[base + N_KEYS:base + 2 * N_KEYS, :], None, PEER_TOPK)
        cand_s = jnp.concatenate([v0[a:a + 1] + v1 for a in range(PEER_TOPK)], axis=0)
        cand_i = jnp.concatenate([i0[a:a + 1] * N_KEYS + i1 for a in range(PEER_TOPK)], axis=0)
        sel_s, sel_i = _top_rows(cand_s, cand_i, PEER_TOPK)
        e = jnp.exp(sel_s - jnp.max(sel_s, axis=0, keepdims=True))
        gate = e / jnp.sum(e, axis=0, keepdims=True)
        idx_ref[h * PEER_TOPK:(h + 1) * PEER_TOPK, :] = sel_i
        gate_ref[h * PEER_TOPK:(h + 1) * PEER_TOPK, :] = gate


def _peer_topk(st):
    rows, n = st.shape
    tl = _pick(n, (256, 128))
    return pl.pallas_call(
        _peer_topk_kernel,
        out_shape=(jax.ShapeDtypeStruct((PEER_SLOTS, n), jnp.int32),
                   jax.ShapeDtypeStruct((PEER_SLOTS, n), F32)),
        grid=(n // tl,),
        in_specs=[pl.BlockSpec((rows, tl), lambda i: (0, i))],
        out_specs=(pl.BlockSpec((PEER_SLOTS, tl), lambda i: (0, i)),
                   pl.BlockSpec((PEER_SLOTS, tl), lambda i: (0, i))),
        compiler_params=_cparams(("parallel",)),
        name="peer_topk",
    )(st)


def _peer_gather_kernel(idx_ref, idx_next_ref, gate_ref, x_ref, g_ref, fg_ref, uv_ref, o_ref,
                        buf_ref, sem_ref, *, tb, final_norm):
    i = pl.program_id(0)
    n = pl.num_programs(0)
    slot = lax.rem(i, 2)
    rows = tb * PEER_SLOTS

    def issue(src_idx_ref, dst_slot):
        def body(r, carry):
            t = r // PEER_SLOTS
            k = r - t * PEER_SLOTS
            e = src_idx_ref[t, k]
            pltpu.make_async_copy(uv_ref.at[pl.ds(e, 1), :],
                                  buf_ref.at[dst_slot, pl.ds(r, 1), :],
                                  sem_ref.at[dst_slot]).start()
            return carry
        lax.fori_loop(0, rows, body, 0)

    @pl.when(i == 0)
    def _():
        issue(idx_ref, 0)

    @pl.when(i + 1 < n)
    def _():
        issue(idx_next_ref, 1 - slot)

    pltpu.make_async_copy(buf_ref.at[slot], buf_ref.at[slot], sem_ref.at[slot]).wait()

    x = x_ref[...]
    xn = _rms(x, g_ref[...])
    lane = lax.broadcasted_iota(jnp.int32, (PEER_SLOTS, tb), 1)
    gates = gate_ref[0]
    outs = []
    for t in range(tb):
        u = buf_ref[slot, t * PEER_SLOTS:(t + 1) * PEER_SLOTS, 0:D_MODEL]
        v = buf_ref[slot, t * PEER_SLOTS:(t + 1) * PEER_SLOTS, D_MODEL:2 * D_MODEL]
        hpre = jnp.sum(u * xn[t:t + 1], axis=-1, keepdims=True)
        act = 0.5 * hpre * (1.0 + lax.erf(hpre * (2.0 ** -0.5)))
        gate_t = jnp.sum(jnp.where(lane == t, gates, 0.0), axis=-1, keepdims=True)
        outs.append(jnp.sum((gate_t * act) * v, axis=0, keepdims=True))
    y = x + jnp.concatenate(outs, axis=0)
    if final_norm:
        y = _rms(y, fg_ref[...])
    o_ref[...] = y


def _peer_gather(x, g, idx, gate, uv, final_g, final_norm):
    n = x.shape[0]
    tb = gate.shape[2]
    nblk = n // tb
    kern = functools.partial(_peer_gather_kernel, tb=tb, final_norm=final_norm)
    return pl.pallas_call(
        kern,
        out_shape=jax.ShapeDtypeStruct((n, D_MODEL), F32),
        grid=(nblk,),
        in_specs=[pl.BlockSpec((tb, PEER_SLOTS), lambda i: (i, 0), memory_space=pltpu.SMEM),
                  pl.BlockSpec((tb, PEER_SLOTS), lambda i: (jnp.minimum(i + 1, nblk - 1), 0),
                               memory_space=pltpu.SMEM),
                  pl.BlockSpec((1, PEER_SLOTS, tb), lambda i: (i, 0, 0)),
                  pl.BlockSpec((tb, D_MODEL), lambda i: (i, 0)),
                  pl.BlockSpec((1, D_MODEL), lambda i: (0, 0)),
                  pl.BlockSpec((1, D_MODEL), lambda i: (0, 0)),
                  pl.BlockSpec(memory_space=pl.ANY)],
        out_specs=pl.BlockSpec((tb, D_MODEL), lambda i: (i, 0)),
        scratch_shapes=[pltpu.VMEM((2, tb * PEER_SLOTS, 2 * D_MODEL), F32),
                        pltpu.SemaphoreType.DMA((2,))],
        compiler_params=_cparams(("arbitrary",)),
        name="peer_gather",
    )(idx, idx, gate, x, g.reshape(1, D_MODEL), final_g.reshape(1, D_MODEL), uv)


PEER_TB = 8


def _peer(x, g, wq, keys, uv, final_g, final_norm):
    n = x.shape[0]
    st = _peer_scores(x, g, wq, keys)
    idx_t, gate_t = _peer_topk(st)
    idx = idx_t.T
    gate = gate_t.reshape(PEER_SLOTS, n // PEER_TB, PEER_TB).transpose(1, 0, 2)
    return _peer_gather(x, g, idx, gate, uv, final_g, final_norm)


def kernel(x_prompt, x_sample, state_conv, state_lru, state_hgrn, cache_k, cache_v, norm1_g, w_in,
           conv_w, conv_b, lru_wa, lru_ba, lru_wx, lru_bx, lru_lambda, hg_lb, hg_norm_g, attn_sink,
           w_branch, w_out, norm2_g, peer_wq, peer_keys, peer_u, peer_v, final_g):
    depth = w_in.shape[0]
    bp, tp, _ = x_prompt.shape
    bs, ts, _ = x_sample.shape
    tpad = SAMPLE_PAD_T
    assert ts <= tpad and tp % WINDOW == 0 and ts >= CONV_W - 1 and cache_k.shape[2] == WINDOW
    n_p = bp * tp
    n_s = bs * tpad

    xs_pad = jnp.pad(x_sample, ((0, 0), (0, tpad - ts), (0, 0)))
    x = jnp.concatenate([x_prompt.reshape(n_p, D_MODEL), xs_pad.reshape(n_s, D_MODEL)], axis=0)

    zeros_conv = jnp.zeros((bp, SUBLANES, MIX_W), F32)
    zeros_h = jnp.zeros((bp, MIX_W), F32)
    zeros_s = jnp.zeros((bp,) + state_hgrn.shape[2:], F32)

    p_st, s_st = [], []
    for l in range(depth):
        lw = dict(conv_w=conv_w[l], conv_b=conv_b[l].reshape(1, MIX_W),
                  wa=lru_wa[l].astype(BF16), ba=lru_ba[l].reshape(1, MIX_W),
                  wx=lru_wx[l].astype(BF16), bx=lru_bx[l].reshape(1, MIX_W),
                  lam=lru_lambda[l].reshape(1, MIX_W))
        proj = _norm_matmul(x, norm1_g[l], w_in[l].astype(BF16))

        ya_p, h_p = _conv_lru(proj, 0, bp, tp, tp, zeros_conv, zeros_h, lw)
        yb_p, st_p = _hgrn(proj, 0, bp, tp, tp, zeros_s, hg_lb, hg_norm_g[l], l)
        yc_p = _swa(proj, 0, bp, tp, attn_sink[l])
        conv0 = jnp.pad(state_conv[l], ((0, 0), (SUBLANES - (CONV_W - 1), 0), (0, 0)))
        ya_s, h_s = _conv_lru(proj, n_p, bs, tpad, ts, conv0, state_lru[l], lw)
        yb_s, st_s = _hgrn(proj, n_p, bs, tpad, ts, state_hgrn[l], hg_lb, hg_norm_g[l], l)
        yc_s = _swa(proj, n_p, bs, tpad, attn_sink[l],
                    cache=(cache_k[l].reshape(bs * WINDOW, KV_W), cache_v[l].reshape(bs * WINDOW, KV_W)))

        ya = jnp.concatenate([ya_p, ya_s], axis=0)
        yb = jnp.concatenate([yb_p, yb_s], axis=0)
        yc = jnp.concatenate([yc_p, yc_s], axis=0)
        z = _branch_mix(ya, yb, yc, proj, w_branch[l].astype(BF16))
        x = _out_proj(z, w_out[l].astype(BF16), x)

        uv = jnp.concatenate([peer_u[l], peer_v[l]], axis=1)
        keys = peer_keys[l].reshape(2 * PEER_HEADS, N_KEYS, PEER_DKEY).astype(BF16)
        x = _peer(x, norm2_g[l], peer_wq[l].astype(BF16), keys, uv, final_g, l == depth - 1)

        pp = proj[:n_p].reshape(bp, tp, IN_COLS)
        ps = proj[n_p:].reshape(bs, tpad, IN_COLS)[:, :ts]
        p_st.append((pp[:, tp - (CONV_W - 1):, COL_XA:COL_XA + MIX_W], h_p, st_p,
                     pp[:, tp - WINDOW:, COL_AK:COL_AK + KV_W].reshape(bp, WINDOW, KV_HEADS, HEAD_DIM),
                     pp[:, tp - WINDOW:, COL_AV:COL_AV + KV_W].reshape(bp, WINDOW, KV_HEADS, HEAD_DIM)))
        k_new = ps[:, :, COL_AK:COL_AK + KV_W].reshape(bs, ts, KV_HEADS, HEAD_DIM)
        v_new = ps[:, :, COL_AV:COL_AV + KV_W].reshape(bs, ts, KV_HEADS, HEAD_DIM)
        s_st.append((ps[:, ts - (CONV_W - 1):, COL_XA:COL_XA + MIX_W], h_s, st_s,
                     jnp.concatenate([cache_k[l][:, ts:], k_new], axis=1),
                     jnp.concatenate([cache_v[l][:, ts:], v_new], axis=1)))

    y_prompt = x[:n_p].reshape(bp, tp, D_MODEL)
    y_sample = x[n_p:].reshape(bs, tpad, D_MODEL)[:, :ts]
    stack = lambda sts, i: jnp.stack([s[i] for s in sts])
    return (y_prompt, y_sample,
            stack(p_st, 0), stack(p_st, 1), stack(p_st, 2), stack(p_st, 3), stack(p_st, 4),
            stack(s_st, 0), stack(s_st, 1), stack(s_st, 2), stack(s_st, 3), stack(s_st, 4))
```

```python
import functools

import jax
import jax.numpy as jnp
from jax import lax
from jax.experimental import pallas as pl
from jax.experimental.pallas import tpu as pltpu

F32 = jnp.float32
BF16 = jnp.bfloat16

D_MODEL = 2048
MIX_W = D_MODEL // 2
N_BRANCH = 3
LRU_BLOCKS = 8
LRU_BD = MIX_W // LRU_BLOCKS
CONV_W = 4
LRU_C = 8.0
HG_HEADS = 8
HG_DK = MIX_W // HG_HEADS
ATT_HEADS = 16
HEAD_DIM = MIX_W // ATT_HEADS
KV_HEADS = 4
GROUP = ATT_HEADS // KV_HEADS
KV_W = KV_HEADS * HEAD_DIM
WINDOW = 128
ATT_SCALE = HEAD_DIM ** -0.5
MASK_VALUE = -1e30
PEER_HEADS = 8
N_KEYS = 128
PEER_TOPK = 16
PEER_DKEY = 128
PEER_SLOTS = PEER_HEADS * PEER_TOPK
EPS = 1e-6

COL_XA = 0
COL_HQ = MIX_W
COL_HF = 2 * MIX_W
COL_HI = 3 * MIX_W
COL_HG = 4 * MIX_W
COL_AQ = 5 * MIX_W
COL_AK = 6 * MIX_W
COL_AV = COL_AK + KV_W
COL_GATE = COL_AV + KV_W
IN_COLS = COL_GATE + N_BRANCH * D_MODEL

SUBLANES = 8
SAMPLE_PAD_T = SUBLANES
VMEM_LIMIT = 48 * 1024 * 1024


def _pick(n, candidates):
    for c in candidates:
        if n % c == 0:
            return c
    raise ValueError(f"no tile in {candidates} divides {n}")


def _cparams(sem, vmem=VMEM_LIMIT):
    return pltpu.CompilerParams(dimension_semantics=sem, vmem_limit_bytes=vmem)


def _rms(x, g):
    return x * lax.rsqrt(jnp.mean(x * x, axis=-1, keepdims=True) + EPS) * g


def _dot_nt(a, b):
    return lax.dot_general(a, b, (((1,), (1,)), ((), ())), preferred_element_type=F32)


def _dot_tn(a, b):
    return lax.dot_general(a, b, (((0,), (0,)), ((), ())), preferred_element_type=F32)


def _norm_matmul_kernel(x_ref, g_ref, w_ref, o_ref, xn_ref):
    @pl.when(pl.program_id(1) == 0)
    def _():
        xn_ref[...] = _rms(x_ref[...], g_ref[...]).astype(BF16)

    o_ref[...] = jnp.dot(xn_ref[...], w_ref[...], preferred_element_type=F32)


def _norm_matmul(x, g, w):
    n, k = x.shape
    c = w.shape[1]
    tm = _pick(n, (1024, 512, 256, 128, 64))
    tn = _pick(c, (512, 256, 128))
    return pl.pallas_call(
        _norm_matmul_kernel,
        out_shape=jax.ShapeDtypeStruct((n, c), F32),
        grid=(n // tm, c // tn),
        in_specs=[pl.BlockSpec((tm, k), lambda i, j: (i, 0)),
                  pl.BlockSpec((1, k), lambda i, j: (0, 0)),
                  pl.BlockSpec((k, tn), lambda i, j: (0, j))],
        out_specs=pl.BlockSpec((tm, tn), lambda i, j: (i, j)),
        scratch_shapes=[pltpu.VMEM((tm, k), BF16)],
        compiler_params=_cparams(("parallel", "arbitrary")),
        name="norm_in_proj",
    )(x, g.reshape(1, k), w)


def _lru_kernel(xa_ref, c0_ref, h0_ref, cw_ref, cb_ref, wa_ref, ba_ref, wx_ref, bx_ref, lam_ref,
                ya_ref, hl_ref, xl_ref, tail_ref, h_ref, *, tc, last_row):
    j = pl.program_id(1)

    @pl.when(j == 0)
    def _():
        tail_ref[...] = c0_ref[0]
        h_ref[...] = h0_ref[0]

    x = xa_ref[...]
    xe = jnp.concatenate([tail_ref[...], x], axis=0)
    y = cb_ref[...] + x * cw_ref[CONV_W - 1:CONV_W, :]
    for d in range(1, CONV_W):
        xs = pltpu.roll(xe, d, axis=0)[SUBLANES:SUBLANES + tc]
        y = y + xs * cw_ref[CONV_W - 1 - d:CONV_W - d, :]
    tail_ref[...] = x[tc - SUBLANES:tc]

    y16 = y.astype(BF16)
    ra, rx = [], []
    for n in range(LRU_BLOCKS):
        yb = y16[:, n * LRU_BD:(n + 1) * LRU_BD]
        ra.append(jnp.dot(yb, wa_ref[n], preferred_element_type=F32))
        rx.append(jnp.dot(yb, wx_ref[n], preferred_element_type=F32))
    r = jax.nn.sigmoid(jnp.concatenate(ra, axis=1) + ba_ref[...])
    ig = jax.nn.sigmoid(jnp.concatenate(rx, axis=1) + bx_ref[...])
    nl = -lam_ref[...]
    softplus = jnp.maximum(nl, 0.0) + jnp.log(1.0 + jnp.exp(-jnp.abs(nl)))
    log_a = -LRU_C * r * softplus
    a = jnp.exp(log_a)
    u = jnp.sqrt(jnp.maximum(1.0 - a * a, 0.0)) * (ig * y)

    rows = lax.broadcasted_iota(jnp.int32, (tc, MIX_W), 0)
    d = 1
    while d < tc:
        keep = rows >= d
        a_sh = jnp.where(keep, pltpu.roll(a, d, axis=0), 1.0)
        u_sh = jnp.where(keep, pltpu.roll(u, d, axis=0), 0.0)
        u = a * u_sh + u
        a = a * a_sh
        d *= 2
    h = a * h_ref[...] + u
    ya_ref[...] = h
    h_ref[...] = h[tc - 1:tc]

    @pl.when(j == pl.num_programs(1) - 1)
    def _():
        hl_ref[0] = h[last_row:last_row + 1]
        xl_ref[0] = x[tc - SUBLANES:tc]


def _conv_lru(proj, row0, nseq, t, t_valid, conv0, h0, lw):
    tc = _pick(t, (256, 128, 64, 32, 16, 8))
    nchunk = t // tc
    blk0 = row0 // tc
    last_row = (t_valid - 1) - (nchunk - 1) * tc
    const2 = lambda b, j: (0, 0)
    const3 = lambda b, j: (0, 0, 0)
    kern = functools.partial(_lru_kernel, tc=tc, last_row=last_row)
    ya, hl, xl = pl.pallas_call(
        kern,
        out_shape=(jax.ShapeDtypeStruct((nseq * t, MIX_W), F32),
                   jax.ShapeDtypeStruct((nseq, 1, MIX_W), F32),
                   jax.ShapeDtypeStruct((nseq, SUBLANES, MIX_W), F32)),
        grid=(nseq, nchunk),
        in_specs=[pl.BlockSpec((tc, MIX_W), lambda b, j: (blk0 + b * nchunk + j, COL_XA // MIX_W)),
                  pl.BlockSpec((1, SUBLANES, MIX_W), lambda b, j: (b, 0, 0)),
                  pl.BlockSpec((1, 1, MIX_W), lambda b, j: (b, 0, 0)),
                  pl.BlockSpec((CONV_W, MIX_W), const2),
                  pl.BlockSpec((1, MIX_W), const2),
                  pl.BlockSpec((LRU_BLOCKS, LRU_BD, LRU_BD), const3),
                  pl.BlockSpec((1, MIX_W), const2),
                  pl.BlockSpec((LRU_BLOCKS, LRU_BD, LRU_BD), const3),
                  pl.BlockSpec((1, MIX_W), const2),
                  pl.BlockSpec((1, MIX_W), const2)],
        out_specs=(pl.BlockSpec((tc, MIX_W), lambda b, j: (b * nchunk + j, 0)),
                   pl.BlockSpec((1, 1, MIX_W), lambda b, j: (b, 0, 0)),
                   pl.BlockSpec((1, SUBLANES, MIX_W), lambda b, j: (b, 0, 0))),
        scratch_shapes=[pltpu.VMEM((SUBLANES, MIX_W), F32), pltpu.VMEM((1, MIX_W), F32)],
        compiler_params=_cparams(("parallel", "arbitrary")),
        name="conv_rglru",
    )(proj, conv0, h0.reshape(nseq, 1, MIX_W), lw["conv_w"], lw["conv_b"], lw["wa"], lw["ba"],
      lw["wx"], lw["bx"], lw["lam"])
    return ya, hl.reshape(nseq, MIX_W), xl


def _hgrn_kernel(q_ref, f_ref, i_ref, g_ref, lb_ref, ng_ref, s0_ref, y_ref, st_out_ref, st_ref,
                 *, cc, t_valid, layer):
    c = pl.program_id(1)
    last = pl.num_programs(1) - 1

    @pl.when(c == 0)
    def _():
        for h in range(HG_HEADS):
            st_ref[h] = s0_ref[0, h].T

    lbp = lb_ref[...]
    e = jnp.exp(lbp - jnp.max(lbp, axis=0, keepdims=True))
    sm = e / jnp.sum(e, axis=0, keepdims=True)
    lower = jnp.zeros((1, MIX_W), F32)
    for i in range(1, layer + 1):
        lower = lower + sm[i:i + 1]

    rows = lax.broadcasted_iota(jnp.int32, (cc, HG_DK), 0)
    ti = lax.broadcasted_iota(jnp.int32, (cc, cc), 0)
    si = lax.broadcasted_iota(jnp.int32, (cc, cc), 1)
    tx = jnp.bitwise_xor(ti, si)
    causal = ti > si

    for h in range(HG_HEADS):
        sl = slice(h * HG_DK, (h + 1) * HG_DK)
        lb = lower[:, sl]
        forget = lb + (1.0 - lb) * jax.nn.sigmoid(f_ref[:, sl])
        g = jnp.log(forget)
        kk = 1.0 - forget
        if t_valid < cc:
            g = jnp.where(rows < t_valid, g, 0.0)
            kk = jnp.where(rows < t_valid, kk, 0.0)
        qh = jax.nn.silu(q_ref[:, sl])
        v16 = i_ref[:, sl].astype(BF16)

        b = g
        d = 1
        while d < cc:
            b = b + jnp.where(rows >= d, pltpu.roll(b, d, axis=0), 0.0)
            d *= 2

        scores = jnp.zeros((cc, cc), F32)
        fm = b
        m = 1
        while m < cc:
            odd = jnp.bitwise_and(rows, m) != 0
            em = jnp.where(rows >= m, pltpu.roll(fm, m, axis=0), 0.0)
            ez = jnp.exp(jnp.where(odd, b - em, fm - b))
            sc = _dot_nt((qh * ez).astype(BF16), (kk * ez).astype(BF16))
            scores = jnp.where((tx >= m) & (tx < 2 * m) & causal, sc, scores)
            fm = jnp.where(odd, fm, pltpu.roll(fm, cc - m, axis=0))
            m *= 2
        scores = jnp.where(ti == si, jnp.sum(qh * kk, axis=1, keepdims=True), scores)

        st = st_ref[h]
        b_last = b[cc - 1:cc]
        o = (_dot_nt((qh * jnp.exp(b)).astype(BF16), st.astype(BF16))
             + jnp.dot(scores.astype(BF16), v16, preferred_element_type=F32))
        st_new = st * jnp.exp(b_last) + _dot_tn(v16, (kk * jnp.exp(b_last - b)).astype(BF16))
        st_ref[h] = st_new

        on = _rms(o, ng_ref[:, sl])
        y_ref[:, sl] = on * jax.nn.silu(g_ref[:, sl])

        @pl.when(c == last)
        def _():
            st_out_ref[0, h] = st_new.T


def _hgrn(proj, row0, nseq, t, t_valid, s0, hg_lb, norm_g, layer):
    cc = _pick(t, (128, 64, 32, 16, 8))
    nchunk = t // cc
    blk0 = row0 // cc
    depth = hg_lb.shape[0]

    def col(c0):
        return pl.BlockSpec((cc, MIX_W), lambda b, j: (blk0 + b * nchunk + j, c0 // MIX_W))

    kern = functools.partial(_hgrn_kernel, cc=cc, t_valid=min(t_valid, cc) if nchunk == 1 else cc,
                             layer=layer)
    yb, st = pl.pallas_call(
        kern,
        out_shape=(jax.ShapeDtypeStruct((nseq * t, MIX_W), F32),
                   jax.ShapeDtypeStruct(s0.shape, F32)),
        grid=(nseq, nchunk),
        in_specs=[col(COL_HQ), col(COL_HF), col(COL_HI), col(COL_HG),
                  pl.BlockSpec((depth, MIX_W), lambda b, j: (0, 0)),
                  pl.BlockSpec((1, MIX_W), lambda b, j: (0, 0)),
                  pl.BlockSpec((1, HG_HEADS, HG_DK, HG_DK), lambda b, j: (b, 0, 0, 0))],
        out_specs=(pl.BlockSpec((cc, MIX_W), lambda b, j: (b * nchunk + j, 0)),
                   pl.BlockSpec((1, HG_HEADS, HG_DK, HG_DK), lambda b, j: (b, 0, 0, 0))),
        scratch_shapes=[pltpu.VMEM((HG_HEADS, HG_DK, HG_DK), F32)],
        compiler_params=_cparams(("parallel", "arbitrary")),
        name="hgrn2",
    )(proj, proj, proj, proj, hg_lb, norm_g.reshape(1, MIX_W), s0)
    return yb, st


def _swa_kernel(sink_ref, q_ref, kc_ref, vc_ref, kp_ref, vp_ref, o_ref, kn_ref, vn_ref,
                *, tq, prev_always):
    i = pl.program_id(1)

    @pl.when(i == pl.num_programs(1) - 1)
    def _():
        kn_ref[0] = kc_ref[...]
        vn_ref[0] = vc_ref[...]

    rows = GROUP * tq
    ri = lax.broadcasted_iota(jnp.int32, (rows, WINDOW), 0)
    ci = lax.broadcasted_iota(jnp.int32, (rows, WINDOW), 1)
    grp = ri // tq
    qi = ri - grp * tq
    if prev_always:
        prev_ok = ci > qi
    else:
        prev_ok = ci > qi + jnp.where(i > 0, 0, WINDOW)
    cur_ok = ci <= qi
    grp1 = grp[:, 0:1]

    q = q_ref[...]
    outs = [None] * ATT_HEADS
    for kh in range(KV_HEADS):
        ks = slice(kh * HEAD_DIM, (kh + 1) * HEAD_DIM)
        kp = kp_ref[:, ks].astype(BF16)
        vp = vp_ref[:, ks].astype(BF16)
        kc = kc_ref[:, ks]
        vc = vc_ref[:, ks]
        if tq < WINDOW:
            pad = jnp.zeros((WINDOW - tq, HEAD_DIM), F32)
            kc = jnp.concatenate([kc, pad], axis=0)
            vc = jnp.concatenate([vc, pad], axis=0)
        kc = kc.astype(BF16)
        vc = vc.astype(BF16)
        qs = jnp.concatenate(
            [q[:, (kh * GROUP + g) * HEAD_DIM:(kh * GROUP + g + 1) * HEAD_DIM] for g in range(GROUP)],
            axis=0).astype(BF16)
        sp = jnp.where(prev_ok, _dot_nt(qs, kp) * ATT_SCALE, MASK_VALUE)
        sc = jnp.where(cur_ok, _dot_nt(qs, kc) * ATT_SCALE, MASK_VALUE)
        sk = jnp.zeros((rows, 1), F32)
        for g in range(GROUP):
            sk = jnp.where(grp1 == g, sink_ref[kh * GROUP + g], sk)
        mx = jnp.maximum(jnp.maximum(jnp.max(sp, axis=-1, keepdims=True),
                                     jnp.max(sc, axis=-1, keepdims=True)), sk)
        ep = jnp.exp(sp - mx)
        ec = jnp.exp(sc - mx)
        den = (jnp.sum(ep, axis=-1, keepdims=True) + jnp.sum(ec, axis=-1, keepdims=True)
               + jnp.exp(sk - mx))
        o = (jnp.dot((ep / den).astype(BF16), vp, preferred_element_type=F32)
             + jnp.dot((ec / den).astype(BF16), vc, preferred_element_type=F32))
        for g in range(GROUP):
            outs[kh * GROUP + g] = o[g * tq:(g + 1) * tq]
    o_ref[...] = jnp.concatenate(outs, axis=1)


def _swa(proj, row0, nseq, t, sink, cache=None):
    tq = _pick(t, (WINDOW, SUBLANES))
    nb = t // tq
    blk0 = row0 // tq
    kcol = COL_AK // KV_W
    vcol = COL_AV // KV_W
    cur = lambda c0: pl.BlockSpec((tq, KV_W), lambda b, j: (blk0 + b * nb + j, c0))
    if cache is None:
        kprev, vprev = proj, proj
        prev = lambda c0: pl.BlockSpec(
            (WINDOW, KV_W), lambda b, j: (blk0 + b * nb + jnp.maximum(j - 1, 0), c0))
        prev_specs = [prev(kcol), prev(vcol)]
    else:
        assert nb == 1
        kprev, vprev = cache
        prev_specs = [pl.BlockSpec((WINDOW, KV_W), lambda b, j: (b, 0))] * 2
    kern = functools.partial(_swa_kernel, tq=tq, prev_always=cache is not None)
    new_kv = jax.ShapeDtypeStruct((nseq, tq, KV_W), F32)
    new_spec = pl.BlockSpec((1, tq, KV_W), lambda b, j: (b, 0, 0))
    return pl.pallas_call(
        kern,
        out_shape=(jax.ShapeDtypeStruct((nseq * t, MIX_W), F32), new_kv, new_kv),
        grid=(nseq, nb),
        in_specs=[pl.BlockSpec(memory_space=pltpu.SMEM),
                  pl.BlockSpec((tq, MIX_W), lambda b, j: (blk0 + b * nb + j, COL_AQ // MIX_W)),
                  cur(kcol), cur(vcol)] + prev_specs,
        out_specs=(pl.BlockSpec((tq, MIX_W), lambda b, j: (b * nb + j, 0)), new_spec, new_spec),
        compiler_params=_cparams(("parallel", "arbitrary")),
        name="swa",
    )(sink, proj, proj, proj, kprev, vprev)


def _mix_kernel(ya_ref, yb_ref, yc_ref, g0_ref, g1_ref, g2_ref, w_ref, z_ref):
    acc = None
    for n, (y_ref, g_ref) in enumerate(((ya_ref, g0_ref), (yb_ref, g1_ref), (yc_ref, g2_ref))):
        p = jnp.dot(y_ref[...].astype(BF16), w_ref[n], preferred_element_type=F32)
        p = jax.nn.sigmoid(g_ref[...]) * p
        acc = p if acc is None else acc + p
    z_ref[...] = acc.astype(BF16)


def _branch_mix(ya, yb, yc, proj, w_br):
    n = ya.shape[0]
    tm = _pick(n, (512, 256, 128, 64))
    tn = 512
    ysp = pl.BlockSpec((tm, MIX_W), lambda i, j: (i, 0))
    gsp = lambda b: pl.BlockSpec((tm, tn), lambda i, j: (i, (COL_GATE + b * D_MODEL) // tn + j))
    return pl.pallas_call(
        _mix_kernel,
        out_shape=jax.ShapeDtypeStruct((n, D_MODEL), BF16),
        grid=(n // tm, D_MODEL // tn),
        in_specs=[ysp, ysp, ysp, gsp(0), gsp(1), gsp(2),
                  pl.BlockSpec((N_BRANCH, MIX_W, tn), lambda i, j: (0, 0, j))],
        out_specs=pl.BlockSpec((tm, tn), lambda i, j: (i, j)),
        compiler_params=_cparams(("parallel", "arbitrary")),
        name="branch_mix",
    )(ya, yb, yc, proj, proj, proj, w_br)


def _out_kernel(z_ref, w_ref, x_ref, o_ref):
    o_ref[...] = x_ref[...] + jnp.dot(z_ref[...], w_ref[...], preferred_element_type=F32)


def _out_proj(z, w, x):
    n = z.shape[0]
    tm = _pick(n, (1024, 512, 256, 128, 64))
    tn = 512
    return pl.pallas_call(
        _out_kernel,
        out_shape=jax.ShapeDtypeStruct((n, D_MODEL), F32),
        grid=(n // tm, D_MODEL // tn),
        in_specs=[pl.BlockSpec((tm, D_MODEL), lambda i, j: (i, 0)),
                  pl.BlockSpec((D_MODEL, tn), lambda i, j: (0, j)),
                  pl.BlockSpec((tm, tn), lambda i, j: (i, j))],
        out_specs=pl.BlockSpec((tm, tn), lambda i, j: (i, j)),
        compiler_params=_cparams(("parallel", "arbitrary")),
        name="out_proj",
    )(z, w, x)


def _peer_score_kernel(x_ref, g_ref, wq_ref, keys_ref, st_ref):
    xn = _rms(x_ref[...], g_ref[...]).astype(BF16)
    q16 = jnp.dot(xn, wq_ref[...], preferred_element_type=F32).astype(BF16)
    for hc in range(2 * PEER_HEADS):
        sl = slice(hc * PEER_DKEY, (hc + 1) * PEER_DKEY)
        st_ref[hc * N_KEYS:(hc + 1) * N_KEYS, :] = _dot_nt(keys_ref[hc], q16[:, sl])


def _peer_scores(x, g, wq, keys):
    n = x.shape[0]
    tm = _pick(n, (256, 128))
    rows = 2 * PEER_HEADS * N_KEYS
    return pl.pallas_call(
        _peer_score_kernel,
        out_shape=jax.ShapeDtypeStruct((rows, n), F32),
        grid=(n // tm,),
        in_specs=[pl.BlockSpec((tm, D_MODEL), lambda i: (i, 0)),
                  pl.BlockSpec((1, D_MODEL), lambda i: (0, 0)),
                  pl.BlockSpec((D_MODEL, rows), lambda i: (0, 0)),
                  pl.BlockSpec((2 * PEER_HEADS, N_KEYS, PEER_DKEY), lambda i: (0, 0, 0))],
        out_specs=pl.BlockSpec((rows, tm), lambda i: (0, i)),
        compiler_params=_cparams(("parallel",)),
        name="peer_scores",
    )(x, g.reshape(1, D_MODEL), wq, keys)


def _top_rows(s, idx_payload, k):
    r, l = s.shape
    riota = lax.broadcasted_iota(jnp.int32, (r, l), 0)
    kiota = lax.broadcasted_iota(jnp.int32, (k, l), 0)
    vals = jnp.zeros((k, l), F32)
    pay = jnp.zeros((k, l), jnp.int32)
    for j in range(k):
        m = jnp.max(s, axis=0, keepdims=True)
        pos = jnp.min(jnp.where(s == m, riota, r), axis=0, keepdims=True)
        hit = riota == pos
        if idx_payload is None:
            p = pos
        else:
            p = jnp.sum(jnp.where(hit, idx_payload, 0), axis=0, keepdims=True)
        s = jnp.where(hit, -jnp.inf, s)
        vals = jnp.where(kiota == j, m, vals)
        pay = jnp.where(kiota == j, p, pay)
    return vals, pay


def _peer_topk_kernel(st_ref, idx_ref, gate_ref):
    for h in range(PEER_HEADS):
        base = h * 2 * N_KEYS
        v0, i0 = _top_rows(st_ref[base:base + N_KEYS, :], None, PEER_TOPK)
        v1, i1 = _top_rows(st_ref[base + N_KEYS:base + 2 * N_KEYS, :], None, PEER_TOPK)
        l = v0.shape[1]
        sub = lax.broadcasted_iota(jnp.int32, (SUBLANES, l), 0)
        cs, ci = [v0[0:1] + v1], [i0[0:1] * N_KEYS + i1]
        for a in range(1, SUBLANES):
            nb = PEER_TOPK // (a + 1)
            cs.append(jnp.where(sub < nb, v0[a:a + 1] + v1[0:SUBLANES], -jnp.inf))
            ci.append(i0[a:a + 1] * N_KEYS + i1[0:SUBLANES])
        cs.append(v0[SUBLANES:PEER_TOPK] + v1[0:1])
        ci.append(i0[SUBLANES:PEER_TOPK] * N_KEYS + i1[0:1])
        sel_s, sel_i = _top_rows(jnp.concatenate(cs, axis=0), jnp.concatenate(ci, axis=0), PEER_TOPK)
        e = jnp.exp(sel_s - jnp.max(sel_s, axis=0, keepdims=True))
        gate = e / jnp.sum(e, axis=0, keepdims=True)
        idx_ref[h * PEER_TOPK:(h + 1) * PEER_TOPK, :] = sel_i
        gate_ref[h * PEER_TOPK:(h + 1) * PEER_TOPK, :] = gate


def _peer_topk(st):
    rows, n = st.shape
    tl = _pick(n, (256, 128))
    return pl.pallas_call(
        _peer_topk_kernel,
        out_shape=(jax.ShapeDtypeStruct((PEER_SLOTS, n), jnp.int32),
                   jax.ShapeDtypeStruct((PEER_SLOTS, n), F32)),
        grid=(n // tl,),
        in_specs=[pl.BlockSpec((rows, tl), lambda i: (0, i))],
        out_specs=(pl.BlockSpec((PEER_SLOTS, tl), lambda i: (0, i)),
                   pl.BlockSpec((PEER_SLOTS, tl), lambda i: (0, i))),
        compiler_params=_cparams(("parallel",)),
        name="peer_topk",
    )(st)


LANES = 128
XROWS = D_MODEL // LANES
SLAB = 2 * XROWS
SLAB_PITCH = SLAB + 4
PEER_TB = 8


def _peer_gather_kernel(idx_ref, idx_next_ref, gate_ref, x_ref, g_ref, fg_ref, uv_ref, o_ref,
                        buf_ref, sem_ref, *, tb, final_norm):
    i = pl.program_id(0)
    n = pl.num_programs(0)
    slot = lax.rem(i, 2)
    nslab = tb * PEER_SLOTS

    def issue(src_idx_ref, s):
        for t in range(tb):
            for k in range(PEER_SLOTS):
                r = t * PEER_SLOTS + k
                pltpu.make_async_copy(uv_ref.at[src_idx_ref[t, k]],
                                      buf_ref.at[s, pl.ds(r * SLAB_PITCH, SLAB), :],
                                      sem_ref.at[s]).start()

    @pl.when(i == 0)
    def _():
        issue(idx_ref, 0)

    for s in range(2):
        @pl.when(jnp.logical_and(i + 1 < n, slot == 1 - s))
        def _():
            issue(idx_next_ref, s)

    filled = buf_ref.at[slot, pl.ds(0, nslab * SLAB), :]
    pltpu.make_async_copy(filled, filled, sem_ref.at[slot]).wait()

    def rms3(v, gain):
        ms = jnp.sum(jnp.sum(v * v, axis=2, keepdims=True), axis=1, keepdims=True) * (1.0 / D_MODEL)
        return v * lax.rsqrt(ms + EPS) * gain

    x = x_ref[...]
    xn = rms3(x, g_ref[...])
    lane = lax.broadcasted_iota(jnp.int32, (PEER_SLOTS, tb), 1)
    gates = gate_ref[0]
    outs = []
    for t in range(tb):
        base = t * PEER_SLOTS * SLAB_PITCH

        def tile(j):
            return buf_ref[slot, pl.ds(base + j, PEER_SLOTS, stride=SLAB_PITCH), :]

        acc = tile(0) * xn[t, 0:1, :]
        for j in range(1, XROWS):
            acc = acc + tile(j) * xn[t, j:j + 1, :]
        hpre = jnp.sum(acc, axis=-1, keepdims=True)
        act = 0.5 * hpre * (1.0 + lax.erf(hpre * (2.0 ** -0.5)))
        gate_t = jnp.sum(jnp.where(lane == t, gates, 0.0), axis=-1, keepdims=True)
        w = gate_t * act
        rows = [jnp.sum(tile(XROWS + j) * w, axis=0, keepdims=True) for j in range(XROWS)]
        outs.append(jnp.concatenate(rows, axis=0)[None])
    y = x + jnp.concatenate(outs, axis=0)
    if final_norm:
        y = rms3(y, fg_ref[...])
    o_ref[...] = y


def _peer_gather(x, g, idx, gate, uv, final_g, final_norm):
    n = x.shape[0]
    tb = gate.shape[2]
    nblk = n // tb
    kern = functools.partial(_peer_gather_kernel, tb=tb, final_norm=final_norm)
    gspec = pl.BlockSpec((XROWS, LANES), lambda i: (0, 0))
    y = pl.pallas_call(
        kern,
        out_shape=jax.ShapeDtypeStruct((n, XROWS, LANES), F32),
        grid=(nblk,),
        in_specs=[pl.BlockSpec((tb, PEER_SLOTS), lambda i: (i, 0), memory_space=pltpu.SMEM),
                  pl.BlockSpec((tb, PEER_SLOTS), lambda i: (jnp.minimum(i + 1, nblk - 1), 0),
                               memory_space=pltpu.SMEM),
                  pl.BlockSpec((1, PEER_SLOTS, tb), lambda i: (i, 0, 0)),
                  pl.BlockSpec((tb, XROWS, LANES), lambda i: (i, 0, 0)),
                  gspec, gspec,
                  pl.BlockSpec(memory_space=pl.ANY)],
        out_specs=pl.BlockSpec((tb, XROWS, LANES), lambda i: (i, 0, 0)),
        scratch_shapes=[pltpu.VMEM((2, tb * PEER_SLOTS * SLAB_PITCH, LANES), F32),
                        pltpu.SemaphoreType.DMA((2,))],
        compiler_params=_cparams(("arbitrary",), vmem=56 * 1024 * 1024),
        name="peer_gather",
    )(idx, idx, gate, x.reshape(n, XROWS, LANES), g.reshape(XROWS, LANES),
      final_g.reshape(XROWS, LANES), uv)
    return y.reshape(n, D_MODEL)


def _peer(x, g, wq, keys, uv, final_g, final_norm):
    n = x.shape[0]
    st = _peer_scores(x, g, wq, keys)
    idx_t, gate_t = _peer_topk(st)
    idx = idx_t.T
    gate = gate_t.reshape(PEER_SLOTS, n // PEER_TB, PEER_TB).transpose(1, 0, 2)
    return _peer_gather(x, g, idx, gate, uv, final_g, final_norm)


def kernel(x_prompt, x_sample, state_conv, state_lru, state_hgrn, cache_k, cache_v, norm1_g, w_in,
           conv_w, conv_b, lru_wa, lru_ba, lru_wx, lru_bx, lru_lambda, hg_lb, hg_norm_g, attn_sink,
           w_branch, w_out, norm2_g, peer_wq, peer_keys, peer_u, peer_v, final_g):
    depth = w_in.shape[0]
    bp, tp, _ = x_prompt.shape
    bs, ts, _ = x_sample.shape
    tpad = SAMPLE_PAD_T
    assert ts <= tpad and tp % WINDOW == 0 and ts >= CONV_W - 1 and cache_k.shape[2] == WINDOW
    n_p = bp * tp
    n_s = bs * tpad

    xs_pad = jnp.pad(x_sample, ((0, 0), (0, tpad - ts), (0, 0)))
    x = jnp.concatenate([x_prompt.reshape(n_p, D_MODEL), xs_pad.reshape(n_s, D_MODEL)], axis=0)

    zeros_conv = jnp.zeros((bp, SUBLANES, MIX_W), F32)
    zeros_h = jnp.zeros((bp, MIX_W), F32)
    zeros_s = jnp.zeros((bp,) + state_hgrn.shape[2:], F32)

    p_st, s_st = [], []
    for l in range(depth):
        lw = dict(conv_w=conv_w[l], conv_b=conv_b[l].reshape(1, MIX_W),
                  wa=lru_wa[l].astype(BF16), ba=lru_ba[l].reshape(1, MIX_W),
                  wx=lru_wx[l].astype(BF16), bx=lru_bx[l].reshape(1, MIX_W),
                  lam=lru_lambda[l].reshape(1, MIX_W))
        proj = _norm_matmul(x, norm1_g[l], w_in[l].astype(BF16))

        ya_p, h_p, xl_p = _conv_lru(proj, 0, bp, tp, tp, zeros_conv, zeros_h, lw)
        yb_p, st_p = _hgrn(proj, 0, bp, tp, tp, zeros_s, hg_lb, hg_norm_g[l], l)
        yc_p, kn_p, vn_p = _swa(proj, 0, bp, tp, attn_sink[l])
        conv0 = jnp.pad(state_conv[l], ((0, 0), (SUBLANES - (CONV_W - 1), 0), (0, 0)))
        ya_s, h_s, xl_s = _conv_lru(proj, n_p, bs, tpad, ts, conv0, state_lru[l], lw)
        yb_s, st_s = _hgrn(proj, n_p, bs, tpad, ts, state_hgrn[l], hg_lb, hg_norm_g[l], l)
        yc_s, kn_s, vn_s = _swa(
            proj, n_p, bs, tpad, attn_sink[l],
            cache=(cache_k[l].reshape(bs * WINDOW, KV_W), cache_v[l].reshape(bs * WINDOW, KV_W)))

        ya = jnp.concatenate([ya_p, ya_s], axis=0)
        yb = jnp.concatenate([yb_p, yb_s], axis=0)
        yc = jnp.concatenate([yc_p, yc_s], axis=0)
        z = _branch_mix(ya, yb, yc, proj, w_branch[l].astype(BF16))
        x = _out_proj(z, w_out[l].astype(BF16), x)

        uv = jnp.concatenate([peer_u[l], peer_v[l]], axis=1).reshape(-1, SLAB, LANES)
        keys = peer_keys[l].reshape(2 * PEER_HEADS, N_KEYS, PEER_DKEY).astype(BF16)
        x = _peer(x, norm2_g[l], peer_wq[l].astype(BF16), keys, uv, final_g, l == depth - 1)

        p_st.append((xl_p[:, SUBLANES - (CONV_W - 1):], h_p, st_p,
                     kn_p.reshape(bp, WINDOW, KV_HEADS, HEAD_DIM),
                     vn_p.reshape(bp, WINDOW, KV_HEADS, HEAD_DIM)))
        k_new = kn_s[:, :ts].reshape(bs, ts, KV_HEADS, HEAD_DIM)
        v_new = vn_s[:, :ts].reshape(bs, ts, KV_HEADS, HEAD_DIM)
        s_st.append((xl_s[:, ts - (CONV_W - 1):ts], h_s, st_s,
                     jnp.concatenate([cache_k[l][:, ts:], k_new], axis=1),
                     jnp.concatenate([cache_v[l][:, ts:], v_new], axis=1)))

    y_prompt = x[:n_p].reshape(bp, tp, D_MODEL)
    y_sample = x[n_p:].reshape(bs, tpad, D_MODEL)[:, :ts]
    stack = lambda sts, i: jnp.stack([s[i] for s in sts])
    return (y_prompt, y_sample,
            stack(p_st, 0), stack(p_st, 1), stack(p_st, 2), stack(p_st, 3), stack(p_st, 4),
            stack(s_st, 0), stack(s_st, 1), stack(s_st, 2), stack(s_st, 3), stack(s_st, 4))
```

```python
import functools

import jax
import jax.numpy as jnp
from jax import lax
from jax.experimental import pallas as pl
from jax.experimental.pallas import tpu as pltpu

F32 = jnp.float32
BF16 = jnp.bfloat16

D_MODEL = 2048
MIX_W = D_MODEL // 2
N_BRANCH = 3
LRU_BLOCKS = 8
LRU_BD = MIX_W // LRU_BLOCKS
CONV_W = 4
LRU_C = 8.0
HG_HEADS = 8
HG_DK = MIX_W // HG_HEADS
ATT_HEADS = 16
HEAD_DIM = MIX_W // ATT_HEADS
KV_HEADS = 4
GROUP = ATT_HEADS // KV_HEADS
KV_W = KV_HEADS * HEAD_DIM
WINDOW = 128
ATT_SCALE = HEAD_DIM ** -0.5
MASK_VALUE = -1e30
PEER_HEADS = 8
N_KEYS = 128
PEER_TOPK = 16
PEER_DKEY = 128
PEER_SLOTS = PEER_HEADS * PEER_TOPK
EPS = 1e-6

COL_XA = 0
COL_HQ = MIX_W
COL_HF = 2 * MIX_W
COL_HI = 3 * MIX_W
COL_HG = 4 * MIX_W
COL_AQ = 5 * MIX_W
COL_AK = 6 * MIX_W
COL_AV = COL_AK + KV_W
COL_GATE = COL_AV + KV_W
IN_COLS = COL_GATE + N_BRANCH * D_MODEL

SUBLANES = 8
SAMPLE_PAD_T = SUBLANES
VMEM_LIMIT = 48 * 1024 * 1024


def _pick(n, candidates):
    for c in candidates:
        if n % c == 0:
            return c
    raise ValueError(f"no tile in {candidates} divides {n}")


def _cparams(sem, vmem=VMEM_LIMIT):
    return pltpu.CompilerParams(dimension_semantics=sem, vmem_limit_bytes=vmem)


def _rms(x, g):
    return x * lax.rsqrt(jnp.mean(x * x, axis=-1, keepdims=True) + EPS) * g


def _dot_nt(a, b):
    return lax.dot_general(a, b, (((1,), (1,)), ((), ())), preferred_element_type=F32)


def _dot_tn(a, b):
    return lax.dot_general(a, b, (((0,), (0,)), ((), ())), preferred_element_type=F32)


def _norm_matmul_kernel(x_ref, g_ref, w_ref, o_ref, xn_ref):
    @pl.when(pl.program_id(1) == 0)
    def _():
        xn_ref[...] = _rms(x_ref[...], g_ref[...]).astype(BF16)

    o_ref[...] = jnp.dot(xn_ref[...], w_ref[...], preferred_element_type=F32)


def _norm_matmul(x, g, w):
    n, k = x.shape
    c = w.shape[1]
    tm = _pick(n, (1024, 512, 256, 128, 64))
    tn = _pick(c, (512, 256, 128))
    return pl.pallas_call(
        _norm_matmul_kernel,
        out_shape=jax.ShapeDtypeStruct((n, c), F32),
        grid=(n // tm, c // tn),
        in_specs=[pl.BlockSpec((tm, k), lambda i, j: (i, 0)),
                  pl.BlockSpec((1, k), lambda i, j: (0, 0)),
                  pl.BlockSpec((k, tn), lambda i, j: (0, j))],
        out_specs=pl.BlockSpec((tm, tn), lambda i, j: (i, j)),
        scratch_shapes=[pltpu.VMEM((tm, k), BF16)],
        compiler_params=_cparams(("parallel", "arbitrary")),
        name="norm_in_proj",
    )(x, g.reshape(1, k), w)


def _lru_kernel(xa_ref, c0_ref, h0_ref, cw_ref, cb_ref, wa_ref, ba_ref, wx_ref, bx_ref, lam_ref,
                ya_ref, hl_ref, xl_ref, tail_ref, h_ref, *, tc, last_row):
    j = pl.program_id(1)

    @pl.when(j == 0)
    def _():
        tail_ref[...] = c0_ref[0]
        h_ref[...] = h0_ref[0]

    x = xa_ref[...]
    xe = jnp.concatenate([tail_ref[...], x], axis=0)
    y = cb_ref[...] + x * cw_ref[CONV_W - 1:CONV_W, :]
    for d in range(1, CONV_W):
        xs = pltpu.roll(xe, d, axis=0)[SUBLANES:SUBLANES + tc]
        y = y + xs * cw_ref[CONV_W - 1 - d:CONV_W - d, :]
    tail_ref[...] = x[tc - SUBLANES:tc]

    y16 = y.astype(BF16)
    ra, rx = [], []
    for n in range(LRU_BLOCKS):
        yb = y16[:, n * LRU_BD:(n + 1) * LRU_BD]
        ra.append(jnp.dot(yb, wa_ref[n], preferred_element_type=F32))
        rx.append(jnp.dot(yb, wx_ref[n], preferred_element_type=F32))
    r = jax.nn.sigmoid(jnp.concatenate(ra, axis=1) + ba_ref[...])
    ig = jax.nn.sigmoid(jnp.concatenate(rx, axis=1) + bx_ref[...])
    nl = -lam_ref[...]
    softplus = jnp.maximum(nl, 0.0) + jnp.log(1.0 + jnp.exp(-jnp.abs(nl)))
    log_a = -LRU_C * r * softplus
    a = jnp.exp(log_a)
    u = jnp.sqrt(jnp.maximum(1.0 - a * a, 0.0)) * (ig * y)

    rows = lax.broadcasted_iota(jnp.int32, (tc, MIX_W), 0)
    d = 1
    while d < tc:
        keep = rows >= d
        a_sh = jnp.where(keep, pltpu.roll(a, d, axis=0), 1.0)
        u_sh = jnp.where(keep, pltpu.roll(u, d, axis=0), 0.0)
        u = a * u_sh + u
        a = a * a_sh
        d *= 2
    h = a * h_ref[...] + u
    ya_ref[...] = h
    h_ref[...] = h[tc - 1:tc]

    @pl.when(j == pl.num_programs(1) - 1)
    def _():
        hl_ref[0] = h[last_row:last_row + 1]
        xl_ref[0] = x[tc - SUBLANES:tc]


def _conv_lru(proj, row0, nseq, t, t_valid, conv0, h0, lw):
    tc = _pick(t, (256, 128, 64, 32, 16, 8))
    nchunk = t // tc
    blk0 = row0 // tc
    last_row = (t_valid - 1) - (nchunk - 1) * tc
    const2 = lambda b, j: (0, 0)
    const3 = lambda b, j: (0, 0, 0)
    kern = functools.partial(_lru_kernel, tc=tc, last_row=last_row)
    ya, hl, xl = pl.pallas_call(
        kern,
        out_shape=(jax.ShapeDtypeStruct((nseq * t, MIX_W), F32),
                   jax.ShapeDtypeStruct((nseq, 1, MIX_W), F32),
                   jax.ShapeDtypeStruct((nseq, SUBLANES, MIX_W), F32)),
        grid=(nseq, nchunk),
        in_specs=[pl.BlockSpec((tc, MIX_W), lambda b, j: (blk0 + b * nchunk + j, COL_XA // MIX_W)),
                  pl.BlockSpec((1, SUBLANES, MIX_W), lambda b, j: (b, 0, 0)),
                  pl.BlockSpec((1, 1, MIX_W), lambda b, j: (b, 0, 0)),
                  pl.BlockSpec((CONV_W, MIX_W), const2),
                  pl.BlockSpec((1, MIX_W), const2),
                  pl.BlockSpec((LRU_BLOCKS, LRU_BD, LRU_BD), const3),
                  pl.BlockSpec((1, MIX_W), const2),
                  pl.BlockSpec((LRU_BLOCKS, LRU_BD, LRU_BD), const3),
                  pl.BlockSpec((1, MIX_W), const2),
                  pl.BlockSpec((1, MIX_W), const2)],
        out_specs=(pl.BlockSpec((tc, MIX_W), lambda b, j: (b * nchunk + j, 0)),
                   pl.BlockSpec((1, 1, MIX_W), lambda b, j: (b, 0, 0)),
                   pl.BlockSpec((1, SUBLANES, MIX_W), lambda b, j: (b, 0, 0))),
        scratch_shapes=[pltpu.VMEM((SUBLANES, MIX_W), F32), pltpu.VMEM((1, MIX_W), F32)],
        compiler_params=_cparams(("parallel", "arbitrary")),
        name="conv_rglru",
    )(proj, conv0, h0.reshape(nseq, 1, MIX_W), lw["conv_w"], lw["conv_b"], lw["wa"], lw["ba"],
      lw["wx"], lw["bx"], lw["lam"])
    return ya, hl.reshape(nseq, MIX_W), xl


def _hgrn_kernel(q_ref, f_ref, i_ref, g_ref, lb_ref, ng_ref, s0_ref, y_ref, st_out_ref, st_ref,
                 *, cc, t_valid, layer):
    c = pl.program_id(1)
    last = pl.num_programs(1) - 1

    @pl.when(c == 0)
    def _():
        for h in range(HG_HEADS):
            st_ref[h] = s0_ref[0, h].T

    lbp = lb_ref[...]
    e = jnp.exp(lbp - jnp.max(lbp, axis=0, keepdims=True))
    sm = e / jnp.sum(e, axis=0, keepdims=True)
    lower = jnp.zeros((1, MIX_W), F32)
    for i in range(1, layer + 1):
        lower = lower + sm[i:i + 1]

    rows = lax.broadcasted_iota(jnp.int32, (cc, HG_DK), 0)
    ti = lax.broadcasted_iota(jnp.int32, (cc, cc), 0)
    si = lax.broadcasted_iota(jnp.int32, (cc, cc), 1)
    tx = jnp.bitwise_xor(ti, si)
    causal = ti > si

    for h in range(HG_HEADS):
        sl = slice(h * HG_DK, (h + 1) * HG_DK)
        lb = lower[:, sl]
        forget = lb + (1.0 - lb) * jax.nn.sigmoid(f_ref[:, sl])
        g = jnp.log(forget)
        kk = 1.0 - forget
        if t_valid < cc:
            g = jnp.where(rows < t_valid, g, 0.0)
            kk = jnp.where(rows < t_valid, kk, 0.0)
        qh = jax.nn.silu(q_ref[:, sl])
        v16 = i_ref[:, sl].astype(BF16)

        b = g
        d = 1
        while d < cc:
            b = b + jnp.where(rows >= d, pltpu.roll(b, d, axis=0), 0.0)
            d *= 2

        scores = jnp.zeros((cc, cc), F32)
        fm = b
        m = 1
        while m < cc:
            odd = jnp.bitwise_and(rows, m) != 0
            em = jnp.where(rows >= m, pltpu.roll(fm, m, axis=0), 0.0)
            ez = jnp.exp(jnp.where(odd, b - em, fm - b))
            sc = _dot_nt((qh * ez).astype(BF16), (kk * ez).astype(BF16))
            scores = jnp.where((tx >= m) & (tx < 2 * m) & causal, sc, scores)
            fm = jnp.where(odd, fm, pltpu.roll(fm, cc - m, axis=0))
            m *= 2
        scores = jnp.where(ti == si, jnp.sum(qh * kk, axis=1, keepdims=True), scores)

        st = st_ref[h]
        b_last = b[cc - 1:cc]
        o = (_dot_nt((qh * jnp.exp(b)).astype(BF16), st.astype(BF16))
             + jnp.dot(scores.astype(BF16), v16, preferred_element_type=F32))
        st_new = st * jnp.exp(b_last) + _dot_tn(v16, (kk * jnp.exp(b_last - b)).astype(BF16))
        st_ref[h] = st_new

        on = _rms(o, ng_ref[:, sl])
        y_ref[:, sl] = on * jax.nn.silu(g_ref[:, sl])

        @pl.when(c == last)
        def _():
            st_out_ref[0, h] = st_new.T


def _hgrn(proj, row0, nseq, t, t_valid, s0, hg_lb, norm_g, layer):
    cc = _pick(t, (128, 64, 32, 16, 8))
    nchunk = t // cc
    blk0 = row0 // cc
    depth = hg_lb.shape[0]

    def col(c0):
        return pl.BlockSpec((cc, MIX_W), lambda b, j: (blk0 + b * nchunk + j, c0 // MIX_W))

    kern = functools.partial(_hgrn_kernel, cc=cc, t_valid=min(t_valid, cc) if nchunk == 1 else cc,
                             layer=layer)
    yb, st = pl.pallas_call(
        kern,
        out_shape=(jax.ShapeDtypeStruct((nseq * t, MIX_W), F32),
                   jax.ShapeDtypeStruct(s0.shape, F32)),
        grid=(nseq, nchunk),
        in_specs=[col(COL_HQ), col(COL_HF), col(COL_HI), col(COL_HG),
                  pl.BlockSpec((depth, MIX_W), lambda b, j: (0, 0)),
                  pl.BlockSpec((1, MIX_W), lambda b, j: (0, 0)),
                  pl.BlockSpec((1, HG_HEADS, HG_DK, HG_DK), lambda b, j: (b, 0, 0, 0))],
        out_specs=(pl.BlockSpec((cc, MIX_W), lambda b, j: (b * nchunk + j, 0)),
                   pl.BlockSpec((1, HG_HEADS, HG_DK, HG_DK), lambda b, j: (b, 0, 0, 0))),
        scratch_shapes=[pltpu.VMEM((HG_HEADS, HG_DK, HG_DK), F32)],
        compiler_params=_cparams(("parallel", "arbitrary")),
        name="hgrn2",
    )(proj, proj, proj, proj, hg_lb, norm_g.reshape(1, MIX_W), s0)
    return yb, st


def _swa_kernel(sink_ref, q_ref, kc_ref, vc_ref, kp_ref, vp_ref, o_ref, kn_ref, vn_ref,
                *, tq, prev_always):
    i = pl.program_id(1)

    @pl.when(i == pl.num_programs(1) - 1)
    def _():
        kn_ref[0] = kc_ref[...]
        vn_ref[0] = vc_ref[...]

    rows = GROUP * tq
    ri = lax.broadcasted_iota(jnp.int32, (rows, WINDOW), 0)
    ci = lax.broadcasted_iota(jnp.int32, (rows, WINDOW), 1)
    grp = ri // tq
    qi = ri - grp * tq
    if prev_always:
        prev_ok = ci > qi
    else:
        prev_ok = ci > qi + jnp.where(i > 0, 0, WINDOW)
    cur_ok = ci <= qi
    grp1 = grp[:, 0:1]

    q = q_ref[...]
    outs = [None] * ATT_HEADS
    for kh in range(KV_HEADS):
        ks = slice(kh * HEAD_DIM, (kh + 1) * HEAD_DIM)
        kp = kp_ref[:, ks].astype(BF16)
        vp = vp_ref[:, ks].astype(BF16)
        kc = kc_ref[:, ks]
        vc = vc_ref[:, ks]
        if tq < WINDOW:
            pad = jnp.zeros((WINDOW - tq, HEAD_DIM), F32)
            kc = jnp.concatenate([kc, pad], axis=0)
            vc = jnp.concatenate([vc, pad], axis=0)
        kc = kc.astype(BF16)
        vc = vc.astype(BF16)
        qs = jnp.concatenate(
            [q[:, (kh * GROUP + g) * HEAD_DIM:(kh * GROUP + g + 1) * HEAD_DIM] for g in range(GROUP)],
            axis=0).astype(BF16)
        sp = jnp.where(prev_ok, _dot_nt(qs, kp) * ATT_SCALE, MASK_VALUE)
        sc = jnp.where(cur_ok, _dot_nt(qs, kc) * ATT_SCALE, MASK_VALUE)
        sk = jnp.zeros((rows, 1), F32)
        for g in range(GROUP):
            sk = jnp.where(grp1 == g, sink_ref[kh * GROUP + g], sk)
        mx = jnp.maximum(jnp.maximum(jnp.max(sp, axis=-1, keepdims=True),
                                     jnp.max(sc, axis=-1, keepdims=True)), sk)
        ep = jnp.exp(sp - mx)
        ec = jnp.exp(sc - mx)
        den = (jnp.sum(ep, axis=-1, keepdims=True) + jnp.sum(ec, axis=-1, keepdims=True)
               + jnp.exp(sk - mx))
        o = (jnp.dot((ep / den).astype(BF16), vp, preferred_element_type=F32)
             + jnp.dot((ec / den).astype(BF16), vc, preferred_element_type=F32))
        for g in range(GROUP):
            outs[kh * GROUP + g] = o[g * tq:(g + 1) * tq]
    o_ref[...] = jnp.concatenate(outs, axis=1)


def _swa(proj, row0, nseq, t, sink, cache=None):
    tq = _pick(t, (WINDOW, SUBLANES))
    nb = t // tq
    blk0 = row0 // tq
    kcol = COL_AK // KV_W
    vcol = COL_AV // KV_W
    cur = lambda c0: pl.BlockSpec((tq, KV_W), lambda b, j: (blk0 + b * nb + j, c0))
    if cache is None:
        kprev, vprev = proj, proj
        prev = lambda c0: pl.BlockSpec(
            (WINDOW, KV_W), lambda b, j: (blk0 + b * nb + jnp.maximum(j - 1, 0), c0))
        prev_specs = [prev(kcol), prev(vcol)]
    else:
        assert nb == 1
        kprev, vprev = cache
        prev_specs = [pl.BlockSpec((WINDOW, KV_W), lambda b, j: (b, 0))] * 2
    kern = functools.partial(_swa_kernel, tq=tq, prev_always=cache is not None)
    new_kv = jax.ShapeDtypeStruct((nseq, tq, KV_W), F32)
    new_spec = pl.BlockSpec((1, tq, KV_W), lambda b, j: (b, 0, 0))
    return pl.pallas_call(
        kern,
        out_shape=(jax.ShapeDtypeStruct((nseq * t, MIX_W), F32), new_kv, new_kv),
        grid=(nseq, nb),
        in_specs=[pl.BlockSpec(memory_space=pltpu.SMEM),
                  pl.BlockSpec((tq, MIX_W), lambda b, j: (blk0 + b * nb + j, COL_AQ // MIX_W)),
                  cur(kcol), cur(vcol)] + prev_specs,
        out_specs=(pl.BlockSpec((tq, MIX_W), lambda b, j: (b * nb + j, 0)), new_spec, new_spec),
        compiler_params=_cparams(("parallel", "arbitrary")),
        name="swa",
    )(sink, proj, proj, proj, kprev, vprev)


def _mix_kernel(ya_ref, yb_ref, yc_ref, g0_ref, g1_ref, g2_ref, w_ref, z_ref):
    acc = None
    for n, (y_ref, g_ref) in enumerate(((ya_ref, g0_ref), (yb_ref, g1_ref), (yc_ref, g2_ref))):
        p = jnp.dot(y_ref[...].astype(BF16), w_ref[n], preferred_element_type=F32)
        p = jax.nn.sigmoid(g_ref[...]) * p
        acc = p if acc is None else acc + p
    z_ref[...] = acc.astype(BF16)


def _branch_mix(ya, yb, yc, proj, w_br):
    n = ya.shape[0]
    tm = _pick(n, (512, 256, 128, 64))
    tn = 512
    ysp = pl.BlockSpec((tm, MIX_W), lambda i, j: (i, 0))
    gsp = lambda b: pl.BlockSpec((tm, tn), lambda i, j: (i, (COL_GATE + b * D_MODEL) // tn + j))
    return pl.pallas_call(
        _mix_kernel,
        out_shape=jax.ShapeDtypeStruct((n, D_MODEL), BF16),
        grid=(n // tm, D_MODEL // tn),
        in_specs=[ysp, ysp, ysp, gsp(0), gsp(1), gsp(2),
                  pl.BlockSpec((N_BRANCH, MIX_W, tn), lambda i, j: (0, 0, j))],
        out_specs=pl.BlockSpec((tm, tn), lambda i, j: (i, j)),
        compiler_params=_cparams(("parallel", "arbitrary")),
        name="branch_mix",
    )(ya, yb, yc, proj, proj, proj, w_br)


def _out_kernel(z_ref, w_ref, x_ref, o_ref):
    o_ref[...] = x_ref[...] + jnp.dot(z_ref[...], w_ref[...], preferred_element_type=F32)


def _out_proj(z, w, x):
    n = z.shape[0]
    tm = _pick(n, (1024, 512, 256, 128, 64))
    tn = 512
    return pl.pallas_call(
        _out_kernel,
        out_shape=jax.ShapeDtypeStruct((n, D_MODEL), F32),
        grid=(n // tm, D_MODEL // tn),
        in_specs=[pl.BlockSpec((tm, D_MODEL), lambda i, j: (i, 0)),
                  pl.BlockSpec((D_MODEL, tn), lambda i, j: (0, j)),
                  pl.BlockSpec((tm, tn), lambda i, j: (i, j))],
        out_specs=pl.BlockSpec((tm, tn), lambda i, j: (i, j)),
        compiler_params=_cparams(("parallel", "arbitrary")),
        name="out_proj",
    )(z, w, x)


def _peer_score_kernel(x_ref, g_ref, wq_ref, keys_ref, st_ref):
    xn = _rms(x_ref[...], g_ref[...]).astype(BF16)
    q16 = jnp.dot(xn, wq_ref[...], preferred_element_type=F32).astype(BF16)
    for hc in range(2 * PEER_HEADS):
        sl = slice(hc * PEER_DKEY, (hc + 1) * PEER_DKEY)
        st_ref[hc * N_KEYS:(hc + 1) * N_KEYS, :] = _dot_nt(keys_ref[hc], q16[:, sl])


def _peer_scores(x, g, wq, keys):
    n = x.shape[0]
    tm = _pick(n, (256, 128))
    rows = 2 * PEER_HEADS * N_KEYS
    return pl.pallas_call(
        _peer_score_kernel,
        out_shape=jax.ShapeDtypeStruct((rows, n), F32),
        grid=(n // tm,),
        in_specs=[pl.BlockSpec((tm, D_MODEL), lambda i: (i, 0)),
                  pl.BlockSpec((1, D_MODEL), lambda i: (0, 0)),
                  pl.BlockSpec((D_MODEL, rows), lambda i: (0, 0)),
                  pl.BlockSpec((2 * PEER_HEADS, N_KEYS, PEER_DKEY), lambda i: (0, 0, 0))],
        out_specs=pl.BlockSpec((rows, tm), lambda i: (0, i)),
        compiler_params=_cparams(("parallel",)),
        name="peer_scores",
    )(x, g.reshape(1, D_MODEL), wq, keys)


def _top_rows(s, idx_payload, k):
    r, l = s.shape
    riota = lax.broadcasted_iota(jnp.int32, (r, l), 0)
    kiota = lax.broadcasted_iota(jnp.int32, (k, l), 0)
    vals = jnp.zeros((k, l), F32)
    pay = jnp.zeros((k, l), jnp.int32)
    for j in range(k):
        m = jnp.max(s, axis=0, keepdims=True)
        pos = jnp.min(jnp.where(s == m, riota, r), axis=0, keepdims=True)
        hit = riota == pos
        if idx_payload is None:
            p = pos
        else:
            p = jnp.sum(jnp.where(hit, idx_payload, 0), axis=0, keepdims=True)
        s = jnp.where(hit, -jnp.inf, s)
        vals = jnp.where(kiota == j, m, vals)
        pay = jnp.where(kiota == j, p, pay)
    return vals, pay


def _peer_topk_kernel(st_ref, idx_ref, gate_ref):
    for h in range(PEER_HEADS):
        base = h * 2 * N_KEYS
        v0, i0 = _top_rows(st_ref[base:base + N_KEYS, :], None, PEER_TOPK)
        v1, i1 = _top_rows(st_ref[base + N_KEYS:base + 2 * N_KEYS, :], None, PEER_TOPK)
        l = v0.shape[1]
        sub = lax.broadcasted_iota(jnp.int32, (SUBLANES, l), 0)
        cs, ci = [v0[0:1] + v1], [i0[0:1] * N_KEYS + i1]
        for a in range(1, SUBLANES):
            nb = PEER_TOPK // (a + 1)
            cs.append(jnp.where(sub < nb, v0[a:a + 1] + v1[0:SUBLANES], -jnp.inf))
            ci.append(i0[a:a + 1] * N_KEYS + i1[0:SUBLANES])
        cs.append(v0[SUBLANES:PEER_TOPK] + v1[0:1])
        ci.append(i0[SUBLANES:PEER_TOPK] * N_KEYS + i1[0:1])
        sel_s, sel_i = _top_rows(jnp.concatenate(cs, axis=0), jnp.concatenate(ci, axis=0), PEER_TOPK)
        e = jnp.exp(sel_s - jnp.max(sel_s, axis=0, keepdims=True))
        gate = e / jnp.sum(e, axis=0, keepdims=True)
        idx_ref[h * PEER_TOPK:(h + 1) * PEER_TOPK, :] = sel_i
        gate_ref[h * PEER_TOPK:(h + 1) * PEER_TOPK, :] = gate


def _peer_topk(st):
    rows, n = st.shape
    tl = _pick(n, (256, 128))
    return pl.pallas_call(
        _peer_topk_kernel,
        out_shape=(jax.ShapeDtypeStruct((PEER_SLOTS, n), jnp.int32),
                   jax.ShapeDtypeStruct((PEER_SLOTS, n), F32)),
        grid=(n // tl,),
        in_specs=[pl.BlockSpec((rows, tl), lambda i: (0, i))],
        out_specs=(pl.BlockSpec((PEER_SLOTS, tl), lambda i: (0, i)),
                   pl.BlockSpec((PEER_SLOTS, tl), lambda i: (0, i))),
        compiler_params=_cparams(("parallel",)),
        name="peer_topk",
    )(st)


LANES = 128
XROWS = D_MODEL // LANES
SLAB_PITCH = XROWS + 1
PEER_TB = 16
HI_MASK = -65536


def _pack_uv(u, v):
    ub = lax.bitcast_convert_type(u.astype(BF16), jnp.uint16).astype(jnp.uint32)
    vb = lax.bitcast_convert_type(v.astype(BF16), jnp.uint16).astype(jnp.uint32)
    return lax.bitcast_convert_type((ub << 16) | vb, jnp.int32).reshape(-1, XROWS, LANES)


def _peer_gather_kernel(idx0_ref, idx1_ref, idx2_ref, gate_ref, x_ref, g_ref, fg_ref, uv_ref, o_ref,
                        buf0_ref, buf1_ref, sem_ref, *, tb, final_norm):
    i = pl.program_id(0)
    last = pl.num_programs(0) - 1
    bufs = (buf0_ref, buf1_ref)

    def start_token(idx_ref, t, b):
        for k in range(PEER_SLOTS):
            r = t * PEER_SLOTS + k
            pltpu.make_async_copy(uv_ref.at[idx_ref[t, k]],
                                  bufs[b].at[pl.ds(r * SLAB_PITCH, XROWS), :],
                                  sem_ref.at[b]).start()

    def wait_buf(b):
        filled = bufs[b].at[pl.ds(0, tb * PEER_SLOTS * XROWS), :]
        pltpu.make_async_copy(filled, filled, sem_ref.at[b]).wait()

    @pl.when(i == 0)
    def _():
        for t in range(tb):
            start_token(idx0_ref, t, 0)

    x = x_ref[...]
    xn = _rms(x, g_ref[...])
    lane = lax.broadcasted_iota(jnp.int32, (PEER_SLOTS, 2 * tb), 1)
    gates = gate_ref[0]
    outs = []
    for half, next_idx_ref in enumerate((idx1_ref, idx2_ref)):
        wait_buf(half)
        for t in range(tb):
            start_token(next_idx_ref, t, 1 - half)
            row = half * tb + t
            base = t * PEER_SLOTS * SLAB_PITCH

            def tile(j):
                return bufs[half][pl.ds(base + j, PEER_SLOTS, stride=SLAB_PITCH), :]

            acc = None
            for j in range(XROWS):
                u = lax.bitcast_convert_type(jnp.bitwise_and(tile(j), HI_MASK), F32)
                p = u * xn[row:row + 1, j * LANES:(j + 1) * LANES]
                acc = p if acc is None else acc + p
            hpre = jnp.sum(acc, axis=-1, keepdims=True)
            act = 0.5 * hpre * (1.0 + lax.erf(hpre * (2.0 ** -0.5)))
            gate_t = jnp.sum(jnp.where(lane == row, gates, 0.0), axis=-1, keepdims=True)
            w = gate_t * act
            rows = []
            for j in range(XROWS):
                v = lax.bitcast_convert_type(jnp.left_shift(tile(j), 16), F32)
                rows.append(jnp.sum(v * w, axis=0, keepdims=True))
            outs.append(jnp.concatenate(rows, axis=1))
    y = x + jnp.concatenate(outs, axis=0)
    if final_norm:
        y = _rms(y, fg_ref[...])
    o_ref[...] = y

    @pl.when(i == last)
    def _():
        wait_buf(0)


def _peer_gather(x, g, idx, gate, uv, final_g, final_norm):
    n = x.shape[0]
    tb = gate.shape[2] // 2
    nblk = n // tb
    assert nblk % 2 == 0
    kern = functools.partial(_peer_gather_kernel, tb=tb, final_norm=final_norm)
    gspec = pl.BlockSpec((1, D_MODEL), lambda i: (0, 0))
    ispec = lambda f: pl.BlockSpec((tb, PEER_SLOTS), lambda i: (f(i), 0), memory_space=pltpu.SMEM)
    buf = pltpu.VMEM((tb * PEER_SLOTS * SLAB_PITCH, LANES), jnp.int32)
    return pl.pallas_call(
        kern,
        out_shape=jax.ShapeDtypeStruct((n, D_MODEL), F32),
        grid=(nblk // 2,),
        in_specs=[ispec(lambda i: 0), ispec(lambda i: 2 * i + 1),
                  ispec(lambda i: jnp.minimum(2 * i + 2, nblk - 1)),
                  pl.BlockSpec((1, PEER_SLOTS, 2 * tb), lambda i: (i, 0, 0)),
                  pl.BlockSpec((2 * tb, D_MODEL), lambda i: (i, 0)),
                  gspec, gspec,
                  pl.BlockSpec(memory_space=pl.ANY)],
        out_specs=pl.BlockSpec((2 * tb, D_MODEL), lambda i: (i, 0)),
        scratch_shapes=[buf, buf, pltpu.SemaphoreType.DMA((2,))],
        compiler_params=_cparams(("arbitrary",), vmem=56 * 1024 * 1024),
        name="peer_gather",
    )(idx, idx, idx, gate, x, g.reshape(1, D_MODEL), final_g.reshape(1, D_MODEL), uv)


def _peer(x, g, wq, keys, uv, final_g, final_norm):
    n = x.shape[0]
    st = _peer_scores(x, g, wq, keys)
    idx_t, gate_t = _peer_topk(st)
    idx = idx_t.T
    gate = gate_t.reshape(PEER_SLOTS, n // (2 * PEER_TB), 2 * PEER_TB).transpose(1, 0, 2)
    return _peer_gather(x, g, idx, gate, uv, final_g, final_norm)


def kernel(x_prompt, x_sample, state_conv, state_lru, state_hgrn, cache_k, cache_v, norm1_g, w_in,
           conv_w, conv_b, lru_wa, lru_ba, lru_wx, lru_bx, lru_lambda, hg_lb, hg_norm_g, attn_sink,
           w_branch, w_out, norm2_g, peer_wq, peer_keys, peer_u, peer_v, final_g):
    depth = w_in.shape[0]
    bp, tp, _ = x_prompt.shape
    bs, ts, _ = x_sample.shape
    tpad = SAMPLE_PAD_T
    assert ts <= tpad and tp % WINDOW == 0 and ts >= CONV_W - 1 and cache_k.shape[2] == WINDOW
    n_p = bp * tp
    n_s = bs * tpad

    xs_pad = jnp.pad(x_sample, ((0, 0), (0, tpad - ts), (0, 0)))
    x = jnp.concatenate([x_prompt.reshape(n_p, D_MODEL), xs_pad.reshape(n_s, D_MODEL)], axis=0)

    zeros_conv = jnp.zeros((bp, SUBLANES, MIX_W), F32)
    zeros_h = jnp.zeros((bp, MIX_W), F32)
    zeros_s = jnp.zeros((bp,) + state_hgrn.shape[2:], F32)

    p_st, s_st = [], []
    for l in range(depth):
        lw = dict(conv_w=conv_w[l], conv_b=conv_b[l].reshape(1, MIX_W),
                  wa=lru_wa[l].astype(BF16), ba=lru_ba[l].reshape(1, MIX_W),
                  wx=lru_wx[l].astype(BF16), bx=lru_bx[l].reshape(1, MIX_W),
                  lam=lru_lambda[l].reshape(1, MIX_W))
        proj = _norm_matmul(x, norm1_g[l], w_in[l].astype(BF16))

        ya_p, h_p, xl_p = _conv_lru(proj, 0, bp, tp, tp, zeros_conv, zeros_h, lw)
        yb_p, st_p = _hgrn(proj, 0, bp, tp, tp, zeros_s, hg_lb, hg_norm_g[l], l)
        yc_p, kn_p, vn_p = _swa(proj, 0, bp, tp, attn_sink[l])
        conv0 = jnp.pad(state_conv[l], ((0, 0), (SUBLANES - (CONV_W - 1), 0), (0, 0)))
        ya_s, h_s, xl_s = _conv_lru(proj, n_p, bs, tpad, ts, conv0, state_lru[l], lw)
        yb_s, st_s = _hgrn(proj, n_p, bs, tpad, ts, state_hgrn[l], hg_lb, hg_norm_g[l], l)
        yc_s, kn_s, vn_s = _swa(
            proj, n_p, bs, tpad, attn_sink[l],
            cache=(cache_k[l].reshape(bs * WINDOW, KV_W), cache_v[l].reshape(bs * WINDOW, KV_W)))

        ya = jnp.concatenate([ya_p, ya_s], axis=0)
        yb = jnp.concatenate([yb_p, yb_s], axis=0)
        yc = jnp.concatenate([yc_p, yc_s], axis=0)
        z = _branch_mix(ya, yb, yc, proj, w_branch[l].astype(BF16))
        x = _out_proj(z, w_out[l].astype(BF16), x)

        uv = _pack_uv(peer_u[l], peer_v[l])
        keys = peer_keys[l].reshape(2 * PEER_HEADS, N_KEYS, PEER_DKEY).astype(BF16)
        x = _peer(x, norm2_g[l], peer_wq[l].astype(BF16), keys, uv, final_g, l == depth - 1)

        p_st.append((xl_p[:, SUBLANES - (CONV_W - 1):], h_p, st_p,
                     kn_p.reshape(bp, WINDOW, KV_HEADS, HEAD_DIM),
                     vn_p.reshape(bp, WINDOW, KV_HEADS, HEAD_DIM)))
        k_new = kn_s[:, :ts].reshape(bs, ts, KV_HEADS, HEAD_DIM)
        v_new = vn_s[:, :ts].reshape(bs, ts, KV_HEADS, HEAD_DIM)
        s_st.append((xl_s[:, ts - (CONV_W - 1):ts], h_s, st_s,
                     jnp.concatenate([cache_k[l][:, ts:], k_new], axis=1),
                     jnp.concatenate([cache_v[l][:, ts:], v_new], axis=1)))

    y_prompt = x[:n_p].reshape(bp, tp, D_MODEL)
    y_sample = x[n_p:].reshape(bs, tpad, D_MODEL)[:, :ts]
    stack = lambda sts, i: jnp.stack([s[i] for s in sts])
    return (y_prompt, y_sample,
            stack(p_st, 0), stack(p_st, 1), stack(p_st, 2), stack(p_st, 3), stack(p_st, 4),
            stack(s_st, 0), stack(s_st, 1), stack(s_st, 2), stack(s_st, 3), stack(s_st, 4))
```

```python
import functools

import jax
import jax.numpy as jnp
from jax import lax
from jax.experimental import pallas as pl
from jax.experimental.pallas import tpu as pltpu

F32 = jnp.float32
BF16 = jnp.bfloat16

D_MODEL = 2048
MIX_W = D_MODEL // 2
N_BRANCH = 3
LRU_BLOCKS = 8
LRU_BD = MIX_W // LRU_BLOCKS
CONV_W = 4
LRU_C = 8.0
HG_HEADS = 8
HG_DK = MIX_W // HG_HEADS
ATT_HEADS = 16
HEAD_DIM = MIX_W // ATT_HEADS
KV_HEADS = 4
GROUP = ATT_HEADS // KV_HEADS
KV_W = KV_HEADS * HEAD_DIM
WINDOW = 128
ATT_SCALE = HEAD_DIM ** -0.5
MASK_VALUE = -1e30
PEER_HEADS = 8
N_KEYS = 128
PEER_TOPK = 16
PEER_DKEY = 128
PEER_SLOTS = PEER_HEADS * PEER_TOPK
EPS = 1e-6

COL_XA = 0
COL_HQ = MIX_W
COL_HF = 2 * MIX_W
COL_HI = 3 * MIX_W
COL_HG = 4 * MIX_W
COL_AQ = 5 * MIX_W
COL_AK = 6 * MIX_W
COL_AV = COL_AK + KV_W
COL_GATE = COL_AV + KV_W
IN_COLS = COL_GATE + N_BRANCH * D_MODEL

SUBLANES = 8
SAMPLE_PAD_T = SUBLANES
VMEM_LIMIT = 48 * 1024 * 1024


def _pick(n, candidates):
    for c in candidates:
        if n % c == 0:
            return c
    raise ValueError(f"no tile in {candidates} divides {n}")


def _cparams(sem, vmem=VMEM_LIMIT):
    return pltpu.CompilerParams(dimension_semantics=sem, vmem_limit_bytes=vmem)


def _rms(x, g):
    return x * lax.rsqrt(jnp.mean(x * x, axis=-1, keepdims=True) + EPS) * g


def _dot_nt(a, b):
    return lax.dot_general(a, b, (((1,), (1,)), ((), ())), preferred_element_type=F32)


def _dot_tn(a, b):
    return lax.dot_general(a, b, (((0,), (0,)), ((), ())), preferred_element_type=F32)


def _norm_matmul_kernel(x_ref, g_ref, w_ref, o_ref, xn_ref):
    @pl.when(pl.program_id(1) == 0)
    def _():
        xn_ref[...] = _rms(x_ref[...], g_ref[...]).astype(BF16)

    o_ref[...] = jnp.dot(xn_ref[...], w_ref[...].astype(BF16), preferred_element_type=F32)


def _norm_matmul(x, g, w):
    n, k = x.shape
    c = w.shape[1]
    tm = _pick(n, (1024, 512, 256, 128, 64))
    tn = _pick(c, (512, 256, 128))
    return pl.pallas_call(
        _norm_matmul_kernel,
        out_shape=jax.ShapeDtypeStruct((n, c), F32),
        grid=(n // tm, c // tn),
        in_specs=[pl.BlockSpec((tm, k), lambda i, j: (i, 0)),
                  pl.BlockSpec((1, k), lambda i, j: (0, 0)),
                  pl.BlockSpec((k, tn), lambda i, j: (0, j))],
        out_specs=pl.BlockSpec((tm, tn), lambda i, j: (i, j)),
        scratch_shapes=[pltpu.VMEM((tm, k), BF16)],
        compiler_params=_cparams(("parallel", "arbitrary")),
        name="norm_in_proj",
    )(x, g.reshape(1, k), w)


def _lru_kernel(xa_ref, c0_ref, h0_ref, cw_ref, cb_ref, wa_ref, ba_ref, wx_ref, bx_ref, lam_ref,
                ya_ref, hl_ref, xl_ref, tail_ref, h_ref, *, tc, last_row):
    j = pl.program_id(1)

    @pl.when(j == 0)
    def _():
        tail_ref[...] = c0_ref[0]
        h_ref[...] = h0_ref[0]

    x = xa_ref[...]
    xe = jnp.concatenate([tail_ref[...], x], axis=0)
    y = cb_ref[...] + x * cw_ref[CONV_W - 1:CONV_W, :]
    for d in range(1, CONV_W):
        xs = pltpu.roll(xe, d, axis=0)[SUBLANES:SUBLANES + tc]
        y = y + xs * cw_ref[CONV_W - 1 - d:CONV_W - d, :]
    tail_ref[...] = x[tc - SUBLANES:tc]

    y16 = y.astype(BF16)
    ra, rx = [], []
    for n in range(LRU_BLOCKS):
        yb = y16[:, n * LRU_BD:(n + 1) * LRU_BD]
        ra.append(jnp.dot(yb, wa_ref[n], preferred_element_type=F32))
        rx.append(jnp.dot(yb, wx_ref[n], preferred_element_type=F32))
    r = jax.nn.sigmoid(jnp.concatenate(ra, axis=1) + ba_ref[...])
    ig = jax.nn.sigmoid(jnp.concatenate(rx, axis=1) + bx_ref[...])
    nl = -lam_ref[...]
    softplus = jnp.maximum(nl, 0.0) + jnp.log(1.0 + jnp.exp(-jnp.abs(nl)))
    log_a = -LRU_C * r * softplus
    a = jnp.exp(log_a)
    u = jnp.sqrt(jnp.maximum(1.0 - a * a, 0.0)) * (ig * y)

    rows = lax.broadcasted_iota(jnp.int32, (tc, MIX_W), 0)
    d = 1
    while d < tc:
        keep = rows >= d
        a_sh = jnp.where(keep, pltpu.roll(a, d, axis=0), 1.0)
        u_sh = jnp.where(keep, pltpu.roll(u, d, axis=0), 0.0)
        u = a * u_sh + u
        a = a * a_sh
        d *= 2
    h = a * h_ref[...] + u
    ya_ref[...] = h
    h_ref[...] = h[tc - 1:tc]

    @pl.when(j == pl.num_programs(1) - 1)
    def _():
        hl_ref[0] = h[last_row:last_row + 1]
        xl_ref[0] = x[tc - SUBLANES:tc]


def _conv_lru(proj, row0, nseq, t, t_valid, conv0, h0, lw):
    tc = _pick(t, (256, 128, 64, 32, 16, 8))
    nchunk = t // tc
    blk0 = row0 // tc
    last_row = (t_valid - 1) - (nchunk - 1) * tc
    const2 = lambda b, j: (0, 0)
    const3 = lambda b, j: (0, 0, 0)
    kern = functools.partial(_lru_kernel, tc=tc, last_row=last_row)
    ya, hl, xl = pl.pallas_call(
        kern,
        out_shape=(jax.ShapeDtypeStruct((nseq * t, MIX_W), F32),
                   jax.ShapeDtypeStruct((nseq, 1, MIX_W), F32),
                   jax.ShapeDtypeStruct((nseq, SUBLANES, MIX_W), F32)),
        grid=(nseq, nchunk),
        in_specs=[pl.BlockSpec((tc, MIX_W), lambda b, j: (blk0 + b * nchunk + j, COL_XA // MIX_W)),
                  pl.BlockSpec((1, SUBLANES, MIX_W), lambda b, j: (b, 0, 0)),
                  pl.BlockSpec((1, 1, MIX_W), lambda b, j: (b, 0, 0)),
                  pl.BlockSpec((CONV_W, MIX_W), const2),
                  pl.BlockSpec((1, MIX_W), const2),
                  pl.BlockSpec((LRU_BLOCKS, LRU_BD, LRU_BD), const3),
                  pl.BlockSpec((1, MIX_W), const2),
                  pl.BlockSpec((LRU_BLOCKS, LRU_BD, LRU_BD), const3),
                  pl.BlockSpec((1, MIX_W), const2),
                  pl.BlockSpec((1, MIX_W), const2)],
        out_specs=(pl.BlockSpec((tc, MIX_W), lambda b, j: (b * nchunk + j, 0)),
                   pl.BlockSpec((1, 1, MIX_W), lambda b, j: (b, 0, 0)),
                   pl.BlockSpec((1, SUBLANES, MIX_W), lambda b, j: (b, 0, 0))),
        scratch_shapes=[pltpu.VMEM((SUBLANES, MIX_W), F32), pltpu.VMEM((1, MIX_W), F32)],
        compiler_params=_cparams(("parallel", "arbitrary")),
        name="conv_rglru",
    )(proj, conv0, h0.reshape(nseq, 1, MIX_W), lw["conv_w"], lw["conv_b"], lw["wa"], lw["ba"],
      lw["wx"], lw["bx"], lw["lam"])
    return ya, hl.reshape(nseq, MIX_W), xl


def _hgrn_kernel(q_ref, f_ref, i_ref, g_ref, lb_ref, ng_ref, s0_ref, y_ref, st_out_ref, st_ref,
                 *, cc, t_valid, layer):
    c = pl.program_id(1)
    last = pl.num_programs(1) - 1

    @pl.when(c == 0)
    def _():
        for h in range(HG_HEADS):
            st_ref[h] = s0_ref[0, h].T

    lbp = lb_ref[...]
    e = jnp.exp(lbp - jnp.max(lbp, axis=0, keepdims=True))
    sm = e / jnp.sum(e, axis=0, keepdims=True)
    lower = jnp.zeros((1, MIX_W), F32)
    for i in range(1, layer + 1):
        lower = lower + sm[i:i + 1]

    rows = lax.broadcasted_iota(jnp.int32, (cc, HG_DK), 0)
    ti = lax.broadcasted_iota(jnp.int32, (cc, cc), 0)
    si = lax.broadcasted_iota(jnp.int32, (cc, cc), 1)
    tx = jnp.bitwise_xor(ti, si)
    causal = ti > si

    for h in range(HG_HEADS):
        sl = slice(h * HG_DK, (h + 1) * HG_DK)
        lb = lower[:, sl]
        forget = lb + (1.0 - lb) * jax.nn.sigmoid(f_ref[:, sl])
        g = jnp.log(forget)
        kk = 1.0 - forget
        if t_valid < cc:
            g = jnp.where(rows < t_valid, g, 0.0)
            kk = jnp.where(rows < t_valid, kk, 0.0)
        qh = jax.nn.silu(q_ref[:, sl])
        v16 = i_ref[:, sl].astype(BF16)

        b = g
        d = 1
        while d < cc:
            b = b + jnp.where(rows >= d, pltpu.roll(b, d, axis=0), 0.0)
            d *= 2

        scores = jnp.zeros((cc, cc), F32)
        fm = b
        m = 1
        while m < cc:
            odd = jnp.bitwise_and(rows, m) != 0
            em = jnp.where(rows >= m, pltpu.roll(fm, m, axis=0), 0.0)
            ez = jnp.exp(jnp.where(odd, b - em, fm - b))
            sc = _dot_nt((qh * ez).astype(BF16), (kk * ez).astype(BF16))
            scores = jnp.where((tx >= m) & (tx < 2 * m) & causal, sc, scores)
            fm = jnp.where(odd, fm, pltpu.roll(fm, cc - m, axis=0))
            m *= 2
        scores = jnp.where(ti == si, jnp.sum(qh * kk, axis=1, keepdims=True), scores)

        st = st_ref[h]
        b_last = b[cc - 1:cc]
        o = (_dot_nt((qh * jnp.exp(b)).astype(BF16), st.astype(BF16))
             + jnp.dot(scores.astype(BF16), v16, preferred_element_type=F32))
        st_new = st * jnp.exp(b_last) + _dot_tn(v16, (kk * jnp.exp(b_last - b)).astype(BF16))
        st_ref[h] = st_new

        on = _rms(o, ng_ref[:, sl])
        y_ref[:, sl] = on * jax.nn.silu(g_ref[:, sl])

    @pl.when(c == last)
    def _():
        for h in range(HG_HEADS):
            st_out_ref[0, h] = st_ref[h].T


def _hgrn(proj, row0, nseq, t, t_valid, s0, hg_lb, norm_g, layer):
    cc = _pick(t, (128, 64, 32, 16, 8))
    nchunk = t // cc
    blk0 = row0 // cc
    depth = hg_lb.shape[0]

    def col(c0):
        return pl.BlockSpec((cc, MIX_W), lambda b, j: (blk0 + b * nchunk + j, c0 // MIX_W))

    kern = functools.partial(_hgrn_kernel, cc=cc, t_valid=min(t_valid, cc) if nchunk == 1 else cc,
                             layer=layer)
    yb, st = pl.pallas_call(
        kern,
        out_shape=(jax.ShapeDtypeStruct((nseq * t, MIX_W), F32),
                   jax.ShapeDtypeStruct(s0.shape, F32)),
        grid=(nseq, nchunk),
        in_specs=[col(COL_HQ), col(COL_HF), col(COL_HI), col(COL_HG),
                  pl.BlockSpec((depth, MIX_W), lambda b, j: (0, 0)),
                  pl.BlockSpec((1, MIX_W), lambda b, j: (0, 0)),
                  pl.BlockSpec((1, HG_HEADS, HG_DK, HG_DK), lambda b, j: (b, 0, 0, 0))],
        out_specs=(pl.BlockSpec((cc, MIX_W), lambda b, j: (b * nchunk + j, 0)),
                   pl.BlockSpec((1, HG_HEADS, HG_DK, HG_DK), lambda b, j: (b, 0, 0, 0))),
        scratch_shapes=[pltpu.VMEM((HG_HEADS, HG_DK, HG_DK), F32)],
        compiler_params=_cparams(("parallel", "arbitrary")),
        name="hgrn2",
    )(proj, proj, proj, proj, hg_lb, norm_g.reshape(1, MIX_W), s0)
    return yb, st


def _swa_kernel(sink_ref, q_ref, kc_ref, vc_ref, kp_ref, vp_ref, o_ref, kn_ref, vn_ref,
                *, tq, prev_always):
    i = pl.program_id(1)

    @pl.when(i == pl.num_programs(1) - 1)
    def _():
        kn_ref[0] = kc_ref[...]
        vn_ref[0] = vc_ref[...]

    rows = GROUP * tq
    ri = lax.broadcasted_iota(jnp.int32, (rows, WINDOW), 0)
    ci = lax.broadcasted_iota(jnp.int32, (rows, WINDOW), 1)
    grp = ri // tq
    qi = ri - grp * tq
    if prev_always:
        prev_ok = ci > qi
    else:
        prev_ok = ci > qi + jnp.where(i > 0, 0, WINDOW)
    cur_ok = ci <= qi
    grp1 = grp[:, 0:1]

    q = q_ref[...]
    outs = [None] * ATT_HEADS
    for kh in range(KV_HEADS):
        ks = slice(kh * HEAD_DIM, (kh + 1) * HEAD_DIM)
        kp = kp_ref[:, ks].astype(BF16)
        vp = vp_ref[:, ks].astype(BF16)
        kc = kc_ref[:, ks]
        vc = vc_ref[:, ks]
        if tq < WINDOW:
            pad = jnp.zeros((WINDOW - tq, HEAD_DIM), F32)
            kc = jnp.concatenate([kc, pad], axis=0)
            vc = jnp.concatenate([vc, pad], axis=0)
        kc = kc.astype(BF16)
        vc = vc.astype(BF16)
        qs = jnp.concatenate(
            [q[:, (kh * GROUP + g) * HEAD_DIM:(kh * GROUP + g + 1) * HEAD_DIM] for g in range(GROUP)],
            axis=0).astype(BF16)
        sp = jnp.where(prev_ok, _dot_nt(qs, kp) * ATT_SCALE, MASK_VALUE)
        sc = jnp.where(cur_ok, _dot_nt(qs, kc) * ATT_SCALE, MASK_VALUE)
        sk = jnp.zeros((rows, 1), F32)
        for g in range(GROUP):
            sk = jnp.where(grp1 == g, sink_ref[kh * GROUP + g], sk)
        mx = jnp.maximum(jnp.maximum(jnp.max(sp, axis=-1, keepdims=True),
                                     jnp.max(sc, axis=-1, keepdims=True)), sk)
        ep = jnp.exp(sp - mx)
        ec = jnp.exp(sc - mx)
        den = (jnp.sum(ep, axis=-1, keepdims=True) + jnp.sum(ec, axis=-1, keepdims=True)
               + jnp.exp(sk - mx))
        o = (jnp.dot((ep / den).astype(BF16), vp, preferred_element_type=F32)
             + jnp.dot((ec / den).astype(BF16), vc, preferred_element_type=F32))
        for g in range(GROUP):
            outs[kh * GROUP + g] = o[g * tq:(g + 1) * tq]
    o_ref[...] = jnp.concatenate(outs, axis=1)


def _swa(proj, row0, nseq, t, sink, cache=None):
    tq = _pick(t, (WINDOW, SUBLANES))
    nb = t // tq
    blk0 = row0 // tq
    kcol = COL_AK // KV_W
    vcol = COL_AV // KV_W
    cur = lambda c0: pl.BlockSpec((tq, KV_W), lambda b, j: (blk0 + b * nb + j, c0))
    if cache is None:
        kprev, vprev = proj, proj
        prev = lambda c0: pl.BlockSpec(
            (WINDOW, KV_W), lambda b, j: (blk0 + b * nb + jnp.maximum(j - 1, 0), c0))
        prev_specs = [prev(kcol), prev(vcol)]
    else:
        assert nb == 1
        kprev, vprev = cache
        prev_specs = [pl.BlockSpec((WINDOW, KV_W), lambda b, j: (b, 0))] * 2
    kern = functools.partial(_swa_kernel, tq=tq, prev_always=cache is not None)
    new_kv = jax.ShapeDtypeStruct((nseq, tq, KV_W), F32)
    new_spec = pl.BlockSpec((1, tq, KV_W), lambda b, j: (b, 0, 0))
    return pl.pallas_call(
        kern,
        out_shape=(jax.ShapeDtypeStruct((nseq * t, MIX_W), F32), new_kv, new_kv),
        grid=(nseq, nb),
        in_specs=[pl.BlockSpec(memory_space=pltpu.SMEM),
                  pl.BlockSpec((tq, MIX_W), lambda b, j: (blk0 + b * nb + j, COL_AQ // MIX_W)),
                  cur(kcol), cur(vcol)] + prev_specs,
        out_specs=(pl.BlockSpec((tq, MIX_W), lambda b, j: (b * nb + j, 0)), new_spec, new_spec),
        compiler_params=_cparams(("parallel", "arbitrary")),
        name="swa",
    )(sink, proj, proj, proj, kprev, vprev)


def _mix_kernel(ya_p_ref, yb_p_ref, yc_p_ref, ya_s_ref, yb_s_ref, yc_s_ref, g0_ref, g1_ref, g2_ref,
                w_ref, z_ref, *, prompt_tiles):
    is_prompt = pl.program_id(0) < prompt_tiles
    acc = None
    for n, (yp_ref, ys_ref, g_ref) in enumerate(((ya_p_ref, ya_s_ref, g0_ref),
                                                 (yb_p_ref, yb_s_ref, g1_ref),
                                                 (yc_p_ref, yc_s_ref, g2_ref))):
        y = jnp.where(is_prompt, yp_ref[...], ys_ref[...])
        p = jnp.dot(y.astype(BF16), w_ref[n], preferred_element_type=F32)
        p = jax.nn.sigmoid(g_ref[...]) * p
        acc = p if acc is None else acc + p
    z_ref[...] = acc.astype(BF16)


def _branch_mix(y_prompt, y_sample, proj, w_br):
    n_p = y_prompt[0].shape[0]
    n_s = y_sample[0].shape[0]
    tm = _pick(n_s, (512, 256, 128, 64))
    assert n_p % tm == 0
    pt = n_p // tm
    tn = 512
    psp = pl.BlockSpec((tm, MIX_W), lambda i, j: (jnp.minimum(i, pt - 1), 0))
    ssp = pl.BlockSpec((tm, MIX_W), lambda i, j: (jnp.maximum(i - pt, 0), 0))
    gsp = lambda b: pl.BlockSpec((tm, tn), lambda i, j: (i, (COL_GATE + b * D_MODEL) // tn + j))
    return pl.pallas_call(
        functools.partial(_mix_kernel, prompt_tiles=pt),
        out_shape=jax.ShapeDtypeStruct((n_p + n_s, D_MODEL), BF16),
        grid=((n_p + n_s) // tm, D_MODEL // tn),
        in_specs=[psp, psp, psp, ssp, ssp, ssp, gsp(0), gsp(1), gsp(2),
                  pl.BlockSpec((N_BRANCH, MIX_W, tn), lambda i, j: (0, 0, j))],
        out_specs=pl.BlockSpec((tm, tn), lambda i, j: (i, j)),
        compiler_params=_cparams(("parallel", "arbitrary")),
        name="branch_mix",
    )(*y_prompt, *y_sample, proj, proj, proj, w_br)


def _out_kernel(z_ref, w_ref, x_ref, o_ref):
    o_ref[...] = x_ref[...] + jnp.dot(z_ref[...], w_ref[...], preferred_element_type=F32)


def _out_proj(z, w, x):
    n = z.shape[0]
    tm = _pick(n, (1024, 512, 256, 128, 64))
    tn = 512
    return pl.pallas_call(
        _out_kernel,
        out_shape=jax.ShapeDtypeStruct((n, D_MODEL), F32),
        grid=(n // tm, D_MODEL // tn),
        in_specs=[pl.BlockSpec((tm, D_MODEL), lambda i, j: (i, 0)),
                  pl.BlockSpec((D_MODEL, tn), lambda i, j: (0, j)),
                  pl.BlockSpec((tm, tn), lambda i, j: (i, j))],
        out_specs=pl.BlockSpec((tm, tn), lambda i, j: (i, j)),
        compiler_params=_cparams(("parallel", "arbitrary")),
        name="out_proj",
    )(z, w, x)


def _peer_score_kernel(x_ref, g_ref, wq_ref, keys_ref, st_ref):
    xn = _rms(x_ref[...], g_ref[...]).astype(BF16)
    q16 = jnp.dot(xn, wq_ref[...], preferred_element_type=F32).astype(BF16)
    for hc in range(2 * PEER_HEADS):
        sl = slice(hc * PEER_DKEY, (hc + 1) * PEER_DKEY)
        st_ref[hc * N_KEYS:(hc + 1) * N_KEYS, :] = _dot_nt(keys_ref[hc], q16[:, sl])


def _peer_scores(x, g, wq, keys):
    n = x.shape[0]
    tm = _pick(n, (256, 128))
    rows = 2 * PEER_HEADS * N_KEYS
    return pl.pallas_call(
        _peer_score_kernel,
        out_shape=jax.ShapeDtypeStruct((rows, n), F32),
        grid=(n // tm,),
        in_specs=[pl.BlockSpec((tm, D_MODEL), lambda i: (i, 0)),
                  pl.BlockSpec((1, D_MODEL), lambda i: (0, 0)),
                  pl.BlockSpec((D_MODEL, rows), lambda i: (0, 0)),
                  pl.BlockSpec((2 * PEER_HEADS, N_KEYS, PEER_DKEY), lambda i: (0, 0, 0))],
        out_specs=pl.BlockSpec((rows, tm), lambda i: (0, i)),
        compiler_params=_cparams(("parallel",)),
        name="peer_scores",
    )(x, g.reshape(1, D_MODEL), wq, keys)


def _top_rows(s, idx_payload, k):
    r, l = s.shape
    riota = lax.broadcasted_iota(jnp.int32, (r, l), 0).astype(F32)
    kiota = lax.broadcasted_iota(jnp.int32, (k, l), 0)
    vals = jnp.zeros((k, l), F32)
    pay = jnp.zeros((k, l), jnp.int32)
    for j in range(k):
        m = jnp.max(s, axis=0, keepdims=True)
        pos = jnp.min(jnp.where(s == m, riota, float(r)), axis=0, keepdims=True)
        hit = riota == pos
        if idx_payload is None:
            p = pos.astype(jnp.int32)
        else:
            p = jnp.sum(jnp.where(hit, idx_payload, 0), axis=0, keepdims=True)
        s = jnp.where(hit, -jnp.inf, s)
        vals = jnp.where(kiota == j, m, vals)
        pay = jnp.where(kiota == j, p, pay)
    return vals, pay


def _peer_topk_kernel(st_ref, idx_ref, gate_ref):
    for h in range(PEER_HEADS):
        base = h * 2 * N_KEYS
        v0, i0 = _top_rows(st_ref[base:base + N_KEYS, :], None, PEER_TOPK)
        v1, i1 = _top_rows(st_ref[base + N_KEYS:base + 2 * N_KEYS, :], None, PEER_TOPK)
        l = v0.shape[1]
        sub = lax.broadcasted_iota(jnp.int32, (SUBLANES, l), 0)
        cs, ci = [v0[0:1] + v1], [i0[0:1] * N_KEYS + i1]
        for a in range(1, SUBLANES):
            nb = PEER_TOPK // (a + 1)
            cs.append(jnp.where(sub < nb, v0[a:a + 1] + v1[0:SUBLANES], -jnp.inf))
            ci.append(i0[a:a + 1] * N_KEYS + i1[0:SUBLANES])
        cs.append(v0[SUBLANES:PEER_TOPK] + v1[0:1])
        ci.append(i0[SUBLANES:PEER_TOPK] * N_KEYS + i1[0:1])
        sel_s, sel_i = _top_rows(jnp.concatenate(cs, axis=0), jnp.concatenate(ci, axis=0), PEER_TOPK)
        e = jnp.exp(sel_s - jnp.max(sel_s, axis=0, keepdims=True))
        gate = e / jnp.sum(e, axis=0, keepdims=True)
        idx_ref[h * PEER_TOPK:(h + 1) * PEER_TOPK, :] = sel_i
        gate_ref[h * PEER_TOPK:(h + 1) * PEER_TOPK, :] = gate


def _peer_topk(st):
    rows, n = st.shape
    tl = _pick(n, (256, 128))
    return pl.pallas_call(
        _peer_topk_kernel,
        out_shape=(jax.ShapeDtypeStruct((PEER_SLOTS, n), jnp.int32),
                   jax.ShapeDtypeStruct((PEER_SLOTS, n), F32)),
        grid=(n // tl,),
        in_specs=[pl.BlockSpec((rows, tl), lambda i: (0, i))],
        out_specs=(pl.BlockSpec((PEER_SLOTS, tl), lambda i: (0, i)),
                   pl.BlockSpec((PEER_SLOTS, tl), lambda i: (0, i))),
        compiler_params=_cparams(("parallel",)),
        name="peer_topk",
    )(st)


LANES = 128
XROWS = D_MODEL // LANES
SLAB_PITCH = XROWS + 1
PEER_TB = 16
HI_MASK = -65536


def _pack_kernel(u_ref, v_ref, o_ref):
    for j in range(XROWS):
        sl = slice(j * LANES, (j + 1) * LANES)
        ub = lax.bitcast_convert_type(u_ref[:, sl].astype(BF16).astype(F32), jnp.int32)
        vb = lax.bitcast_convert_type(v_ref[:, sl].astype(BF16).astype(F32), jnp.int32)
        o_ref[:, j, :] = jnp.bitwise_or(ub, lax.shift_right_logical(vb, 16))


def _pack_uv(u, v):
    e = u.shape[0]
    te = _pick(e, (256, 128))
    spec = pl.BlockSpec((te, D_MODEL), lambda i: (i, 0))
    return pl.pallas_call(
        _pack_kernel,
        out_shape=jax.ShapeDtypeStruct((e, XROWS, LANES), jnp.int32),
        grid=(e // te,),
        in_specs=[spec, spec],
        out_specs=pl.BlockSpec((te, XROWS, LANES), lambda i: (i, 0, 0)),
        compiler_params=_cparams(("parallel",)),
        name="peer_pack",
    )(u, v)


def _peer_gather_kernel(idx0_ref, idx1_ref, idx2_ref, gate_ref, x_ref, g_ref, fg_ref, uv_ref, o_ref,
                        buf0_ref, buf1_ref, sem_ref, *, tb, final_norm):
    i = pl.program_id(0)
    last = pl.num_programs(0) - 1
    bufs = (buf0_ref, buf1_ref)

    def start_token(idx_ref, t, b):
        for k in range(PEER_SLOTS):
            r = t * PEER_SLOTS + k
            pltpu.make_async_copy(uv_ref.at[idx_ref[t, k]],
                                  bufs[b].at[pl.ds(r * SLAB_PITCH, XROWS), :],
                                  sem_ref.at[b]).start()

    def wait_buf(b):
        filled = bufs[b].at[pl.ds(0, tb * PEER_SLOTS * XROWS), :]
        pltpu.make_async_copy(filled, filled, sem_ref.at[b]).wait()

    @pl.when(i == 0)
    def _():
        for t in range(tb):
            start_token(idx0_ref, t, 0)

    x = x_ref[...]
    xn = _rms(x, g_ref[...])
    lane = lax.broadcasted_iota(jnp.int32, (PEER_SLOTS, 2 * tb), 1)
    gates = gate_ref[0]
    outs = []
    for half, next_idx_ref in enumerate((idx1_ref, idx2_ref)):
        wait_buf(half)
        for t in range(tb):
            start_token(next_idx_ref, t, 1 - half)
            row = half * tb + t
            base = t * PEER_SLOTS * SLAB_PITCH

            def tile(j):
                return bufs[half][pl.ds(base + j, PEER_SLOTS, stride=SLAB_PITCH), :]

            acc = None
            for j in range(XROWS):
                u = lax.bitcast_convert_type(jnp.bitwise_and(tile(j), HI_MASK), F32)
                p = u * xn[row:row + 1, j * LANES:(j + 1) * LANES]
                acc = p if acc is None else acc + p
            hpre = jnp.sum(acc, axis=-1, keepdims=True)
            act = 0.5 * hpre * (1.0 + lax.erf(hpre * (2.0 ** -0.5)))
            gate_t = jnp.sum(jnp.where(lane == row, gates, 0.0), axis=-1, keepdims=True)
            w = gate_t * act
            rows = []
            for j in range(XROWS):
                v = lax.bitcast_convert_type(jnp.left_shift(tile(j), 16), F32)
                rows.append(jnp.sum(v * w, axis=0, keepdims=True))
            outs.append(jnp.concatenate(rows, axis=1))
    y = x + jnp.concatenate(outs, axis=0)
    if final_norm:
        y = _rms(y, fg_ref[...])
    o_ref[...] = y

    @pl.when(i == last)
    def _():
        wait_buf(0)


def _peer_gather(x, g, idx, gate, uv, final_g, final_norm):
    n = x.shape[0]
    tb = gate.shape[2] // 2
    nblk = n // tb
    assert nblk % 2 == 0
    kern = functools.partial(_peer_gather_kernel, tb=tb, final_norm=final_norm)
    gspec = pl.BlockSpec((1, D_MODEL), lambda i: (0, 0))
    ispec = lambda f: pl.BlockSpec((tb, PEER_SLOTS), lambda i: (f(i), 0), memory_space=pltpu.SMEM)
    buf = pltpu.VMEM((tb * PEER_SLOTS * SLAB_PITCH, LANES), jnp.int32)
    return pl.pallas_call(
        kern,
        out_shape=jax.ShapeDtypeStruct((n, D_MODEL), F32),
        grid=(nblk // 2,),
        in_specs=[ispec(lambda i: 0), ispec(lambda i: 2 * i + 1),
                  ispec(lambda i: jnp.minimum(2 * i + 2, nblk - 1)),
                  pl.BlockSpec((1, PEER_SLOTS, 2 * tb), lambda i: (i, 0, 0)),
                  pl.BlockSpec((2 * tb, D_MODEL), lambda i: (i, 0)),
                  gspec, gspec,
                  pl.BlockSpec(memory_space=pl.ANY)],
        out_specs=pl.BlockSpec((2 * tb, D_MODEL), lambda i: (i, 0)),
        scratch_shapes=[buf, buf, pltpu.SemaphoreType.DMA((2,))],
        compiler_params=_cparams(("arbitrary",), vmem=56 * 1024 * 1024),
        name="peer_gather",
    )(idx, idx, idx, gate, x, g.reshape(1, D_MODEL), final_g.reshape(1, D_MODEL), uv)


def _peer(x, g, wq, keys, uv, final_g, final_norm):
    n = x.shape[0]
    st = _peer_scores(x, g, wq, keys)
    idx_t, gate_t = _peer_topk(st)
    idx = idx_t.T
    gate = gate_t.reshape(PEER_SLOTS, n // (2 * PEER_TB), 2 * PEER_TB).transpose(1, 0, 2)
    return _peer_gather(x, g, idx, gate, uv, final_g, final_norm)


def kernel(x_prompt, x_sample, state_conv, state_lru, state_hgrn, cache_k, cache_v, norm1_g, w_in,
           conv_w, conv_b, lru_wa, lru_ba, lru_wx, lru_bx, lru_lambda, hg_lb, hg_norm_g, attn_sink,
           w_branch, w_out, norm2_g, peer_wq, peer_keys, peer_u, peer_v, final_g):
    depth = w_in.shape[0]
    bp, tp, _ = x_prompt.shape
    bs, ts, _ = x_sample.shape
    tpad = SAMPLE_PAD_T
    assert ts <= tpad and tp % WINDOW == 0 and ts >= CONV_W - 1 and cache_k.shape[2] == WINDOW
    n_p = bp * tp
    n_s = bs * tpad

    xs_pad = jnp.pad(x_sample, ((0, 0), (0, tpad - ts), (0, 0)))
    x = jnp.concatenate([x_prompt.reshape(n_p, D_MODEL), xs_pad.reshape(n_s, D_MODEL)], axis=0)

    zeros_conv = jnp.zeros((bp, SUBLANES, MIX_W), F32)
    zeros_h = jnp.zeros((bp, MIX_W), F32)
    zeros_s = jnp.zeros((bp,) + state_hgrn.shape[2:], F32)

    p_st, s_st = [], []
    for l in range(depth):
        lw = dict(conv_w=conv_w[l], conv_b=conv_b[l].reshape(1, MIX_W),
                  wa=lru_wa[l].astype(BF16), ba=lru_ba[l].reshape(1, MIX_W),
                  wx=lru_wx[l].astype(BF16), bx=lru_bx[l].reshape(1, MIX_W),
                  lam=lru_lambda[l].reshape(1, MIX_W))
        proj = _norm_matmul(x, norm1_g[l], w_in[l])

        ya_p, h_p, xl_p = _conv_lru(proj, 0, bp, tp, tp, zeros_conv, zeros_h, lw)
        yb_p, st_p = _hgrn(proj, 0, bp, tp, tp, zeros_s, hg_lb, hg_norm_g[l], l)
        yc_p, kn_p, vn_p = _swa(proj, 0, bp, tp, attn_sink[l])
        conv0 = jnp.pad(state_conv[l], ((0, 0), (SUBLANES - (CONV_W - 1), 0), (0, 0)))
        ya_s, h_s, xl_s = _conv_lru(proj, n_p, bs, tpad, ts, conv0, state_lru[l], lw)
        yb_s, st_s = _hgrn(proj, n_p, bs, tpad, ts, state_hgrn[l], hg_lb, hg_norm_g[l], l)
        yc_s, kn_s, vn_s = _swa(
            proj, n_p, bs, tpad, attn_sink[l],
            cache=(cache_k[l].reshape(bs * WINDOW, KV_W), cache_v[l].reshape(bs * WINDOW, KV_W)))

        z = _branch_mix((ya_p, yb_p, yc_p), (ya_s, yb_s, yc_s), proj, w_branch[l].astype(BF16))
        x = _out_proj(z, w_out[l].astype(BF16), x)

        uv = _pack_uv(peer_u[l], peer_v[l])
        keys = peer_keys[l].reshape(2 * PEER_HEADS, N_KEYS, PEER_DKEY).astype(BF16)
        x = _peer(x, norm2_g[l], peer_wq[l].astype(BF16), keys, uv, final_g, l == depth - 1)

        p_st.append((xl_p[:, SUBLANES - (CONV_W - 1):], h_p, st_p,
                     kn_p.reshape(bp, WINDOW, KV_HEADS, HEAD_DIM),
                     vn_p.reshape(bp, WINDOW, KV_HEADS, HEAD_DIM)))
        k_new = kn_s[:, :ts].reshape(bs, ts, KV_HEADS, HEAD_DIM)
        v_new = vn_s[:, :ts].reshape(bs, ts, KV_HEADS, HEAD_DIM)
        s_st.append((xl_s[:, ts - (CONV_W - 1):ts], h_s, st_s,
                     jnp.concatenate([cache_k[l][:, ts:], k_new], axis=1),
                     jnp.concatenate([cache_v[l][:, ts:], v_new], axis=1)))

    y_prompt = x[:n_p].reshape(bp, tp, D_MODEL)
    y_sample = x[n_p:].reshape(bs, tpad, D_MODEL)[:, :ts]
    stack = lambda sts, i: jnp.stack([s[i] for s in sts])
    return (y_prompt, y_sample,
            stack(p_st, 0), stack(p_st, 1), stack(p_st, 2), stack(p_st, 3), stack(p_st, 4),
            stack(s_st, 0), stack(s_st, 1), stack(s_st, 2), stack(s_st, 3), stack(s_st, 4))
```

```python
import functools

import jax
import jax.numpy as jnp
from jax import lax
from jax.experimental import pallas as pl
from jax.experimental.pallas import tpu as pltpu

F32 = jnp.float32
BF16 = jnp.bfloat16

D_MODEL = 2048
MIX_W = D_MODEL // 2
N_BRANCH = 3
LRU_BLOCKS = 8
LRU_BD = MIX_W // LRU_BLOCKS
CONV_W = 4
LRU_C = 8.0
HG_HEADS = 8
HG_DK = MIX_W // HG_HEADS
ATT_HEADS = 16
HEAD_DIM = MIX_W // ATT_HEADS
KV_HEADS = 4
GROUP = ATT_HEADS // KV_HEADS
KV_W = KV_HEADS * HEAD_DIM
WINDOW = 128
ATT_SCALE = HEAD_DIM ** -0.5
MASK_VALUE = -1e30
PEER_HEADS = 8
N_KEYS = 128
PEER_TOPK = 16
PEER_DKEY = 128
PEER_SLOTS = PEER_HEADS * PEER_TOPK
EPS = 1e-6

COL_XA = 0
COL_HQ = MIX_W
COL_HF = 2 * MIX_W
COL_HI = 3 * MIX_W
COL_HG = 4 * MIX_W
COL_AQ = 5 * MIX_W
COL_AK = 6 * MIX_W
COL_AV = COL_AK + KV_W
COL_GATE = COL_AV + KV_W
IN_COLS = COL_GATE + N_BRANCH * D_MODEL

SUBLANES = 8
SAMPLE_PAD_T = SUBLANES
VMEM_LIMIT = 48 * 1024 * 1024


def _pick(n, candidates):
    for c in candidates:
        if n % c == 0:
            return c
    raise ValueError(f"no tile in {candidates} divides {n}")


def _cparams(sem, vmem=VMEM_LIMIT):
    return pltpu.CompilerParams(dimension_semantics=sem, vmem_limit_bytes=vmem)


def _rms(x, g):
    return x * lax.rsqrt(jnp.mean(x * x, axis=-1, keepdims=True) + EPS) * g


def _dot_nt(a, b):
    return lax.dot_general(a, b, (((1,), (1,)), ((), ())), preferred_element_type=F32)


def _dot_tn(a, b):
    return lax.dot_general(a, b, (((0,), (0,)), ((), ())), preferred_element_type=F32)


def _norm_matmul_kernel(x_ref, g_ref, w_ref, o_ref, xn_ref):
    @pl.when(pl.program_id(1) == 0)
    def _():
        xn_ref[...] = _rms(x_ref[...], g_ref[...]).astype(BF16)

    o_ref[...] = jnp.dot(xn_ref[...], w_ref[...].astype(BF16), preferred_element_type=F32)


def _norm_matmul(x, g, w):
    n, k = x.shape
    c = w.shape[1]
    tm = _pick(n, (1024, 512, 256, 128, 64))
    tn = _pick(c, (512, 256, 128))
    return pl.pallas_call(
        _norm_matmul_kernel,
        out_shape=jax.ShapeDtypeStruct((n, c), F32),
        grid=(n // tm, c // tn),
        in_specs=[pl.BlockSpec((tm, k), lambda i, j: (i, 0)),
                  pl.BlockSpec((1, k), lambda i, j: (0, 0)),
                  pl.BlockSpec((k, tn), lambda i, j: (0, j))],
        out_specs=pl.BlockSpec((tm, tn), lambda i, j: (i, j)),
        scratch_shapes=[pltpu.VMEM((tm, k), BF16)],
        compiler_params=_cparams(("parallel", "arbitrary")),
        name="norm_in_proj",
    )(x, g.reshape(1, k), w)


def _lru_kernel(xa_ref, c0_ref, h0_ref, cw_ref, cb_ref, wa_ref, ba_ref, wx_ref, bx_ref, lam_ref,
                ya_ref, hl_ref, xl_ref, tail_ref, h_ref, *, tc, last_row):
    j = pl.program_id(1)

    @pl.when(j == 0)
    def _():
        tail_ref[...] = c0_ref[0]
        h_ref[...] = h0_ref[0]

    x = xa_ref[...]
    xe = jnp.concatenate([tail_ref[...], x], axis=0)
    y = cb_ref[...] + x * cw_ref[CONV_W - 1:CONV_W, :]
    for d in range(1, CONV_W):
        xs = pltpu.roll(xe, d, axis=0)[SUBLANES:SUBLANES + tc]
        y = y + xs * cw_ref[CONV_W - 1 - d:CONV_W - d, :]
    tail_ref[...] = x[tc - SUBLANES:tc]

    y16 = y.astype(BF16)
    ra, rx = [], []
    for n in range(LRU_BLOCKS):
        yb = y16[:, n * LRU_BD:(n + 1) * LRU_BD]
        ra.append(jnp.dot(yb, wa_ref[n], preferred_element_type=F32))
        rx.append(jnp.dot(yb, wx_ref[n], preferred_element_type=F32))
    r = jax.nn.sigmoid(jnp.concatenate(ra, axis=1) + ba_ref[...])
    ig = jax.nn.sigmoid(jnp.concatenate(rx, axis=1) + bx_ref[...])
    nl = -lam_ref[...]
    softplus = jnp.maximum(nl, 0.0) + jnp.log(1.0 + jnp.exp(-jnp.abs(nl)))
    log_a = -LRU_C * r * softplus
    a = jnp.exp(log_a)
    u = jnp.sqrt(jnp.maximum(1.0 - a * a, 0.0)) * (ig * y)

    rows = lax.broadcasted_iota(jnp.int32, (tc, MIX_W), 0)
    d = 1
    while d < tc:
        keep = rows >= d
        a_sh = jnp.where(keep, pltpu.roll(a, d, axis=0), 1.0)
        u_sh = jnp.where(keep, pltpu.roll(u, d, axis=0), 0.0)
        u = a * u_sh + u
        a = a * a_sh
        d *= 2
    h = a * h_ref[...] + u
    ya_ref[...] = h
    h_ref[...] = h[tc - 1:tc]

    @pl.when(j == pl.num_programs(1) - 1)
    def _():
        hl_ref[0] = h[last_row:last_row + 1]
        xl_ref[0] = x[tc - SUBLANES:tc]


def _conv_lru(proj, row0, nseq, t, t_valid, conv0, h0, lw):
    tc = _pick(t, (256, 128, 64, 32, 16, 8))
    nchunk = t // tc
    blk0 = row0 // tc
    last_row = (t_valid - 1) - (nchunk - 1) * tc
    const2 = lambda b, j: (0, 0)
    const3 = lambda b, j: (0, 0, 0)
    kern = functools.partial(_lru_kernel, tc=tc, last_row=last_row)
    ya, hl, xl = pl.pallas_call(
        kern,
        out_shape=(jax.ShapeDtypeStruct((nseq * t, MIX_W), F32),
                   jax.ShapeDtypeStruct((nseq, 1, MIX_W), F32),
                   jax.ShapeDtypeStruct((nseq, SUBLANES, MIX_W), F32)),
        grid=(nseq, nchunk),
        in_specs=[pl.BlockSpec((tc, MIX_W), lambda b, j: (blk0 + b * nchunk + j, COL_XA // MIX_W)),
                  pl.BlockSpec((1, SUBLANES, MIX_W), lambda b, j: (b, 0, 0)),
                  pl.BlockSpec((1, 1, MIX_W), lambda b, j: (b, 0, 0)),
                  pl.BlockSpec((CONV_W, MIX_W), const2),
                  pl.BlockSpec((1, MIX_W), const2),
                  pl.BlockSpec((LRU_BLOCKS, LRU_BD, LRU_BD), const3),
                  pl.BlockSpec((1, MIX_W), const2),
                  pl.BlockSpec((LRU_BLOCKS, LRU_BD, LRU_BD), const3),
                  pl.BlockSpec((1, MIX_W), const2),
                  pl.BlockSpec((1, MIX_W), const2)],
        out_specs=(pl.BlockSpec((tc, MIX_W), lambda b, j: (b * nchunk + j, 0)),
                   pl.BlockSpec((1, 1, MIX_W), lambda b, j: (b, 0, 0)),
                   pl.BlockSpec((1, SUBLANES, MIX_W), lambda b, j: (b, 0, 0))),
        scratch_shapes=[pltpu.VMEM((SUBLANES, MIX_W), F32), pltpu.VMEM((1, MIX_W), F32)],
        compiler_params=_cparams(("parallel", "arbitrary")),
        name="conv_rglru",
    )(proj, conv0, h0.reshape(nseq, 1, MIX_W), lw["conv_w"], lw["conv_b"], lw["wa"], lw["ba"],
      lw["wx"], lw["bx"], lw["lam"])
    return ya, hl.reshape(nseq, MIX_W), xl


def _hgrn_kernel(q_ref, f_ref, i_ref, g_ref, lb_ref, ng_ref, s0_ref, y_ref, st_out_ref, st_ref,
                 *, cc, t_valid, layer):
    c = pl.program_id(1)
    last = pl.num_programs(1) - 1

    @pl.when(c == 0)
    def _():
        for h in range(HG_HEADS):
            st_ref[h] = s0_ref[0, h].T

    lbp = lb_ref[...]
    e = jnp.exp(lbp - jnp.max(lbp, axis=0, keepdims=True))
    sm = e / jnp.sum(e, axis=0, keepdims=True)
    lower = jnp.zeros((1, MIX_W), F32)
    for i in range(1, layer + 1):
        lower = lower + sm[i:i + 1]

    rows = lax.broadcasted_iota(jnp.int32, (cc, HG_DK), 0)
    ti = lax.broadcasted_iota(jnp.int32, (cc, cc), 0)
    si = lax.broadcasted_iota(jnp.int32, (cc, cc), 1)
    tx = jnp.bitwise_xor(ti, si)
    causal = ti > si

    for h in range(HG_HEADS):
        sl = slice(h * HG_DK, (h + 1) * HG_DK)
        lb = lower[:, sl]
        forget = lb + (1.0 - lb) * jax.nn.sigmoid(f_ref[:, sl])
        g = jnp.log(forget)
        kk = 1.0 - forget
        if t_valid < cc:
            g = jnp.where(rows < t_valid, g, 0.0)
            kk = jnp.where(rows < t_valid, kk, 0.0)
        qh = jax.nn.silu(q_ref[:, sl])
        v16 = i_ref[:, sl].astype(BF16)

        b = g
        d = 1
        while d < cc:
            b = b + jnp.where(rows >= d, pltpu.roll(b, d, axis=0), 0.0)
            d *= 2

        scores = jnp.zeros((cc, cc), F32)
        fm = b
        m = 1
        while m < cc:
            odd = jnp.bitwise_and(rows, m) != 0
            em = jnp.where(rows >= m, pltpu.roll(fm, m, axis=0), 0.0)
            ez = jnp.exp(jnp.where(odd, b - em, fm - b))
            sc = _dot_nt((qh * ez).astype(BF16), (kk * ez).astype(BF16))
            scores = jnp.where((tx >= m) & (tx < 2 * m) & causal, sc, scores)
            fm = jnp.where(odd, fm, pltpu.roll(fm, cc - m, axis=0))
            m *= 2
        scores = jnp.where(ti == si, jnp.sum(qh * kk, axis=1, keepdims=True), scores)

        st = st_ref[h]
        b_last = b[cc - 1:cc]
        o = (_dot_nt((qh * jnp.exp(b)).astype(BF16), st.astype(BF16))
             + jnp.dot(scores.astype(BF16), v16, preferred_element_type=F32))
        st_new = st * jnp.exp(b_last) + _dot_tn(v16, (kk * jnp.exp(b_last - b)).astype(BF16))
        st_ref[h] = st_new

        on = _rms(o, ng_ref[:, sl])
        y_ref[:, sl] = on * jax.nn.silu(g_ref[:, sl])

    @pl.when(c == last)
    def _():
        for h in range(HG_HEADS):
            st_out_ref[0, h] = st_ref[h].T


def _hgrn(proj, row0, nseq, t, t_valid, s0, hg_lb, norm_g, layer):
    cc = _pick(t, (128, 64, 32, 16, 8))
    nchunk = t // cc
    blk0 = row0 // cc
    depth = hg_lb.shape[0]

    def col(c0):
        return pl.BlockSpec((cc, MIX_W), lambda b, j: (blk0 + b * nchunk + j, c0 // MIX_W))

    kern = functools.partial(_hgrn_kernel, cc=cc, t_valid=min(t_valid, cc) if nchunk == 1 else cc,
                             layer=layer)
    yb, st = pl.pallas_call(
        kern,
        out_shape=(jax.ShapeDtypeStruct((nseq * t, MIX_W), F32),
                   jax.ShapeDtypeStruct(s0.shape, F32)),
        grid=(nseq, nchunk),
        in_specs=[col(COL_HQ), col(COL_HF), col(COL_HI), col(COL_HG),
                  pl.BlockSpec((depth, MIX_W), lambda b, j: (0, 0)),
                  pl.BlockSpec((1, MIX_W), lambda b, j: (0, 0)),
                  pl.BlockSpec((1, HG_HEADS, HG_DK, HG_DK), lambda b, j: (b, 0, 0, 0))],
        out_specs=(pl.BlockSpec((cc, MIX_W), lambda b, j: (b * nchunk + j, 0)),
                   pl.BlockSpec((1, HG_HEADS, HG_DK, HG_DK), lambda b, j: (b, 0, 0, 0))),
        scratch_shapes=[pltpu.VMEM((HG_HEADS, HG_DK, HG_DK), F32)],
        compiler_params=_cparams(("parallel", "arbitrary")),
        name="hgrn2",
    )(proj, proj, proj, proj, hg_lb, norm_g.reshape(1, MIX_W), s0)
    return yb, st


def _swa_kernel(sink_ref, q_ref, kc_ref, vc_ref, kp_ref, vp_ref, o_ref, kn_ref, vn_ref,
                *, tq, prev_always):
    i = pl.program_id(1)

    @pl.when(i == pl.num_programs(1) - 1)
    def _():
        kn_ref[0] = kc_ref[...]
        vn_ref[0] = vc_ref[...]

    rows = GROUP * tq
    ri = lax.broadcasted_iota(jnp.int32, (rows, WINDOW), 0)
    ci = lax.broadcasted_iota(jnp.int32, (rows, WINDOW), 1)
    grp = ri // tq
    qi = ri - grp * tq
    if prev_always:
        prev_ok = ci > qi
    else:
        prev_ok = ci > qi + jnp.where(i > 0, 0, WINDOW)
    cur_ok = ci <= qi
    grp1 = grp[:, 0:1]

    q = q_ref[...]
    outs = [None] * ATT_HEADS
    for kh in range(KV_HEADS):
        ks = slice(kh * HEAD_DIM, (kh + 1) * HEAD_DIM)
        kp = kp_ref[:, ks].astype(BF16)
        vp = vp_ref[:, ks].astype(BF16)
        kc = kc_ref[:, ks]
        vc = vc_ref[:, ks]
        if tq < WINDOW:
            pad = jnp.zeros((WINDOW - tq, HEAD_DIM), F32)
            kc = jnp.concatenate([kc, pad], axis=0)
            vc = jnp.concatenate([vc, pad], axis=0)
        kc = kc.astype(BF16)
        vc = vc.astype(BF16)
        qs = jnp.concatenate(
            [q[:, (kh * GROUP + g) * HEAD_DIM:(kh * GROUP + g + 1) * HEAD_DIM] for g in range(GROUP)],
            axis=0).astype(BF16)
        sp = jnp.where(prev_ok, _dot_nt(qs, kp) * ATT_SCALE, MASK_VALUE)
        sc = jnp.where(cur_ok, _dot_nt(qs, kc) * ATT_SCALE, MASK_VALUE)
        sk = jnp.zeros((rows, 1), F32)
        for g in range(GROUP):
            sk = jnp.where(grp1 == g, sink_ref[kh * GROUP + g], sk)
        mx = jnp.maximum(jnp.maximum(jnp.max(sp, axis=-1, keepdims=True),
                                     jnp.max(sc, axis=-1, keepdims=True)), sk)
        ep = jnp.exp(sp - mx)
        ec = jnp.exp(sc - mx)
        den = (jnp.sum(ep, axis=-1, keepdims=True) + jnp.sum(ec, axis=-1, keepdims=True)
               + jnp.exp(sk - mx))
        o = (jnp.dot((ep / den).astype(BF16), vp, preferred_element_type=F32)
             + jnp.dot((ec / den).astype(BF16), vc, preferred_element_type=F32))
        for g in range(GROUP):
            outs[kh * GROUP + g] = o[g * tq:(g + 1) * tq]
    o_ref[...] = jnp.concatenate(outs, axis=1)


def _swa(proj, row0, nseq, t, sink, cache=None):
    tq = _pick(t, (WINDOW, SUBLANES))
    nb = t // tq
    blk0 = row0 // tq
    kcol = COL_AK // KV_W
    vcol = COL_AV // KV_W
    cur = lambda c0: pl.BlockSpec((tq, KV_W), lambda b, j: (blk0 + b * nb + j, c0))
    if cache is None:
        kprev, vprev = proj, proj
        prev = lambda c0: pl.BlockSpec(
            (WINDOW, KV_W), lambda b, j: (blk0 + b * nb + jnp.maximum(j - 1, 0), c0))
        prev_specs = [prev(kcol), prev(vcol)]
    else:
        assert nb == 1
        kprev, vprev = cache
        prev_specs = [pl.BlockSpec((WINDOW, KV_W), lambda b, j: (b, 0))] * 2
    kern = functools.partial(_swa_kernel, tq=tq, prev_always=cache is not None)
    new_kv = jax.ShapeDtypeStruct((nseq, tq, KV_W), F32)
    new_spec = pl.BlockSpec((1, tq, KV_W), lambda b, j: (b, 0, 0))
    return pl.pallas_call(
        kern,
        out_shape=(jax.ShapeDtypeStruct((nseq * t, MIX_W), F32), new_kv, new_kv),
        grid=(nseq, nb),
        in_specs=[pl.BlockSpec(memory_space=pltpu.SMEM),
                  pl.BlockSpec((tq, MIX_W), lambda b, j: (blk0 + b * nb + j, COL_AQ // MIX_W)),
                  cur(kcol), cur(vcol)] + prev_specs,
        out_specs=(pl.BlockSpec((tq, MIX_W), lambda b, j: (b * nb + j, 0)), new_spec, new_spec),
        compiler_params=_cparams(("parallel", "arbitrary")),
        name="swa",
    )(sink, proj, proj, proj, kprev, vprev)


def _mix_kernel(ya_p_ref, yb_p_ref, yc_p_ref, ya_s_ref, yb_s_ref, yc_s_ref, g0_ref, g1_ref, g2_ref,
                w_ref, z_ref, *, prompt_tiles):
    is_prompt = pl.program_id(0) < prompt_tiles
    acc = None
    for n, (yp_ref, ys_ref, g_ref) in enumerate(((ya_p_ref, ya_s_ref, g0_ref),
                                                 (yb_p_ref, yb_s_ref, g1_ref),
                                                 (yc_p_ref, yc_s_ref, g2_ref))):
        y = jnp.where(is_prompt, yp_ref[...], ys_ref[...])
        p = jnp.dot(y.astype(BF16), w_ref[n], preferred_element_type=F32)
        p = jax.nn.sigmoid(g_ref[...]) * p
        acc = p if acc is None else acc + p
    z_ref[...] = acc.astype(BF16)


def _branch_mix(y_prompt, y_sample, proj, w_br):
    n_p = y_prompt[0].shape[0]
    n_s = y_sample[0].shape[0]
    tm = _pick(n_s, (512, 256, 128, 64))
    assert n_p % tm == 0
    pt = n_p // tm
    tn = 512
    psp = pl.BlockSpec((tm, MIX_W), lambda i, j: (jnp.minimum(i, pt - 1), 0))
    ssp = pl.BlockSpec((tm, MIX_W), lambda i, j: (jnp.maximum(i - pt, 0), 0))
    gsp = lambda b: pl.BlockSpec((tm, tn), lambda i, j: (i, (COL_GATE + b * D_MODEL) // tn + j))
    return pl.pallas_call(
        functools.partial(_mix_kernel, prompt_tiles=pt),
        out_shape=jax.ShapeDtypeStruct((n_p + n_s, D_MODEL), BF16),
        grid=((n_p + n_s) // tm, D_MODEL // tn),
        in_specs=[psp, psp, psp, ssp, ssp, ssp, gsp(0), gsp(1), gsp(2),
                  pl.BlockSpec((N_BRANCH, MIX_W, tn), lambda i, j: (0, 0, j))],
        out_specs=pl.BlockSpec((tm, tn), lambda i, j: (i, j)),
        compiler_params=_cparams(("parallel", "arbitrary")),
        name="branch_mix",
    )(*y_prompt, *y_sample, proj, proj, proj, w_br)


def _out_kernel(z_ref, w_ref, x_ref, o_ref):
    o_ref[...] = x_ref[...] + jnp.dot(z_ref[...], w_ref[...], preferred_element_type=F32)


def _out_proj(z, w, x):
    n = z.shape[0]
    tm = _pick(n, (1024, 512, 256, 128, 64))
    tn = 512
    return pl.pallas_call(
        _out_kernel,
        out_shape=jax.ShapeDtypeStruct((n, D_MODEL), F32),
        grid=(n // tm, D_MODEL // tn),
        in_specs=[pl.BlockSpec((tm, D_MODEL), lambda i, j: (i, 0)),
                  pl.BlockSpec((D_MODEL, tn), lambda i, j: (0, j)),
                  pl.BlockSpec((tm, tn), lambda i, j: (i, j))],
        out_specs=pl.BlockSpec((tm, tn), lambda i, j: (i, j)),
        compiler_params=_cparams(("parallel", "arbitrary")),
        name="out_proj",
    )(z, w, x)


def _peer_score_kernel(x_ref, g_ref, wq_ref, keys_ref, st_ref):
    xn = _rms(x_ref[...], g_ref[...]).astype(BF16)
    q16 = jnp.dot(xn, wq_ref[...], preferred_element_type=F32).astype(BF16)
    for hc in range(2 * PEER_HEADS):
        sl = slice(hc * PEER_DKEY, (hc + 1) * PEER_DKEY)
        st_ref[hc * N_KEYS:(hc + 1) * N_KEYS, :] = _dot_nt(keys_ref[hc], q16[:, sl])


def _peer_scores(x, g, wq, keys):
    n = x.shape[0]
    tm = _pick(n, (256, 128))
    rows = 2 * PEER_HEADS * N_KEYS
    return pl.pallas_call(
        _peer_score_kernel,
        out_shape=jax.ShapeDtypeStruct((rows, n), F32),
        grid=(n // tm,),
        in_specs=[pl.BlockSpec((tm, D_MODEL), lambda i: (i, 0)),
                  pl.BlockSpec((1, D_MODEL), lambda i: (0, 0)),
                  pl.BlockSpec((D_MODEL, rows), lambda i: (0, 0)),
                  pl.BlockSpec((2 * PEER_HEADS, N_KEYS, PEER_DKEY), lambda i: (0, 0, 0))],
        out_specs=pl.BlockSpec((rows, tm), lambda i: (0, i)),
        compiler_params=_cparams(("parallel",)),
        name="peer_scores",
    )(x, g.reshape(1, D_MODEL), wq, keys)


def _top_rows(s, idx_payload, k):
    r, l = s.shape
    riota = lax.broadcasted_iota(jnp.int32, (r, l), 0).astype(F32)
    kiota = lax.broadcasted_iota(jnp.int32, (k, l), 0)
    vals = jnp.zeros((k, l), F32)
    pay = jnp.zeros((k, l), jnp.int32)
    for j in range(k):
        m = jnp.max(s, axis=0, keepdims=True)
        pos = jnp.min(jnp.where(s == m, riota, float(r)), axis=0, keepdims=True)
        hit = riota == pos
        if idx_payload is None:
            p = pos.astype(jnp.int32)
        else:
            p = jnp.sum(jnp.where(hit, idx_payload, 0), axis=0, keepdims=True)
        s = jnp.where(hit, -jnp.inf, s)
        vals = jnp.where(kiota == j, m, vals)
        pay = jnp.where(kiota == j, p, pay)
    return vals, pay


def _peer_topk_kernel(st_ref, idx_ref, gate_ref):
    for h in range(PEER_HEADS):
        base = h * 2 * N_KEYS
        v0, i0 = _top_rows(st_ref[base:base + N_KEYS, :], None, PEER_TOPK)
        v1, i1 = _top_rows(st_ref[base + N_KEYS:base + 2 * N_KEYS, :], None, PEER_TOPK)
        l = v0.shape[1]
        sub = lax.broadcasted_iota(jnp.int32, (SUBLANES, l), 0)
        cs, ci = [v0[0:1] + v1], [i0[0:1] * N_KEYS + i1]
        for a in range(1, SUBLANES):
            nb = PEER_TOPK // (a + 1)
            cs.append(jnp.where(sub < nb, v0[a:a + 1] + v1[0:SUBLANES], -jnp.inf))
            ci.append(i0[a:a + 1] * N_KEYS + i1[0:SUBLANES])
        cs.append(v0[SUBLANES:PEER_TOPK] + v1[0:1])
        ci.append(i0[SUBLANES:PEER_TOPK] * N_KEYS + i1[0:1])
        sel_s, sel_i = _top_rows(jnp.concatenate(cs, axis=0), jnp.concatenate(ci, axis=0), PEER_TOPK)
        e = jnp.exp(sel_s - jnp.max(sel_s, axis=0, keepdims=True))
        gate = e / jnp.sum(e, axis=0, keepdims=True)
        idx_ref[h * PEER_TOPK:(h + 1) * PEER_TOPK, :] = sel_i
        gate_ref[h * PEER_TOPK:(h + 1) * PEER_TOPK, :] = gate


def _peer_topk(st):
    rows, n = st.shape
    tl = _pick(n, (256, 128))
    return pl.pallas_call(
        _peer_topk_kernel,
        out_shape=(jax.ShapeDtypeStruct((PEER_SLOTS, n), jnp.int32),
                   jax.ShapeDtypeStruct((PEER_SLOTS, n), F32)),
        grid=(n // tl,),
        in_specs=[pl.BlockSpec((rows, tl), lambda i: (0, i))],
        out_specs=(pl.BlockSpec((PEER_SLOTS, tl), lambda i: (0, i)),
                   pl.BlockSpec((PEER_SLOTS, tl), lambda i: (0, i))),
        compiler_params=_cparams(("parallel",)),
        name="peer_topk",
    )(st)


LANES = 128
XROWS = D_MODEL // LANES
SLAB_PITCH = XROWS + 1
PEER_TB = 16
HI_MASK = -65536


def _pack_kernel(u_ref, v_ref, o_ref):
    for j in range(XROWS):
        sl = slice(j * LANES, (j + 1) * LANES)
        ub = lax.bitcast_convert_type(u_ref[:, sl].astype(BF16).astype(F32), jnp.int32)
        vb = lax.bitcast_convert_type(v_ref[:, sl].astype(BF16).astype(F32), jnp.int32)
        o_ref[:, j, :] = jnp.bitwise_or(ub, lax.shift_right_logical(vb, 16))


def _pack_uv(u, v):
    e = u.shape[0]
    te = _pick(e, (256, 128))
    spec = pl.BlockSpec((te, D_MODEL), lambda i: (i, 0))
    return pl.pallas_call(
        _pack_kernel,
        out_shape=jax.ShapeDtypeStruct((e, XROWS, LANES), jnp.int32),
        grid=(e // te,),
        in_specs=[spec, spec],
        out_specs=pl.BlockSpec((te, XROWS, LANES), lambda i: (i, 0, 0)),
        compiler_params=_cparams(("parallel",)),
        name="peer_pack",
    )(u, v)


def _peer_gather_kernel(idx0_ref, idx1_ref, idx2_ref, gate_ref, x_ref, g_ref, fg_ref, uv_ref, o_ref,
                        buf0_ref, buf1_ref, sem_ref, *, tb, final_norm):
    i = pl.program_id(0)
    last = pl.num_programs(0) - 1
    bufs = (buf0_ref, buf1_ref)

    def start_token(idx_ref, t, b):
        for k in range(PEER_SLOTS):
            r = t * PEER_SLOTS + k
            pltpu.async_copy(uv_ref.at[idx_ref[t, k]],
                             bufs[b].at[pl.ds(r * SLAB_PITCH, XROWS), :],
                             sem_ref.at[b], priority=k % 2)

    def wait_buf(b):
        filled = bufs[b].at[pl.ds(0, tb * PEER_SLOTS * XROWS), :]
        pltpu.make_async_copy(filled, filled, sem_ref.at[b]).wait()

    @pl.when(i == 0)
    def _():
        for t in range(tb):
            start_token(idx0_ref, t, 0)

    x = x_ref[...]
    xn = _rms(x, g_ref[...])
    lane = lax.broadcasted_iota(jnp.int32, (PEER_SLOTS, 2 * tb), 1)
    gates = gate_ref[0]
    outs = []
    for half, next_idx_ref in enumerate((idx1_ref, idx2_ref)):
        wait_buf(half)
        for t in range(tb):
            start_token(next_idx_ref, t, 1 - half)
            row = half * tb + t
            base = t * PEER_SLOTS * SLAB_PITCH

            def tile(j):
                return bufs[half][pl.ds(base + j, PEER_SLOTS, stride=SLAB_PITCH), :]

            acc = None
            for j in range(XROWS):
                u = lax.bitcast_convert_type(jnp.bitwise_and(tile(j), HI_MASK), F32)
                p = u * xn[row:row + 1, j * LANES:(j + 1) * LANES]
                acc = p if acc is None else acc + p
            hpre = jnp.sum(acc, axis=-1, keepdims=True)
            act = 0.5 * hpre * (1.0 + lax.erf(hpre * (2.0 ** -0.5)))
            gate_t = jnp.sum(jnp.where(lane == row, gates, 0.0), axis=-1, keepdims=True)
            w = gate_t * act
            rows = []
            for j in range(XROWS):
                v = lax.bitcast_convert_type(jnp.left_shift(tile(j), 16), F32)
                rows.append(jnp.sum(v * w, axis=0, keepdims=True))
            outs.append(jnp.concatenate(rows, axis=1))
    y = x + jnp.concatenate(outs, axis=0)
    if final_norm:
        y = _rms(y, fg_ref[...])
    o_ref[...] = y

    @pl.when(i == last)
    def _():
        wait_buf(0)


def _peer_gather(x, g, idx, gate, uv, final_g, final_norm, row0, nrows):
    tb = gate.shape[2] // 2
    nblk = nrows // tb
    b0 = row0 // tb
    assert nrows % (2 * tb) == 0 and row0 % (2 * tb) == 0
    kern = functools.partial(_peer_gather_kernel, tb=tb, final_norm=final_norm)
    gspec = pl.BlockSpec((1, D_MODEL), lambda i: (0, 0))
    ispec = lambda f: pl.BlockSpec((tb, PEER_SLOTS), lambda i: (b0 + f(i), 0), memory_space=pltpu.SMEM)
    buf = pltpu.VMEM((tb * PEER_SLOTS * SLAB_PITCH, LANES), jnp.int32)
    return pl.pallas_call(
        kern,
        out_shape=jax.ShapeDtypeStruct((nrows, D_MODEL), F32),
        grid=(nblk // 2,),
        in_specs=[ispec(lambda i: 0), ispec(lambda i: 2 * i + 1),
                  ispec(lambda i: jnp.minimum(2 * i + 2, nblk - 1)),
                  pl.BlockSpec((1, PEER_SLOTS, 2 * tb), lambda i: (b0 // 2 + i, 0, 0)),
                  pl.BlockSpec((2 * tb, D_MODEL), lambda i: (b0 // 2 + i, 0)),
                  gspec, gspec,
                  pl.BlockSpec(memory_space=pl.ANY)],
        out_specs=pl.BlockSpec((2 * tb, D_MODEL), lambda i: (i, 0)),
        scratch_shapes=[buf, buf, pltpu.SemaphoreType.DMA((2,))],
        compiler_params=_cparams(("arbitrary",), vmem=56 * 1024 * 1024),
        name="peer_gather",
    )(idx, idx, idx, gate, x, g.reshape(1, D_MODEL), final_g.reshape(1, D_MODEL), uv)


def _peer(x, g, wq, keys, uv, final_g, final_norm, splits):
    n = x.shape[0]
    st = _peer_scores(x, g, wq, keys)
    idx_t, gate_t = _peer_topk(st)
    idx = idx_t.T
    gate = gate_t.reshape(PEER_SLOTS, n // (2 * PEER_TB), 2 * PEER_TB).transpose(1, 0, 2)
    return [_peer_gather(x, g, idx, gate, uv, final_g, final_norm, r0, nr) for r0, nr in splits]


def kernel(x_prompt, x_sample, state_conv, state_lru, state_hgrn, cache_k, cache_v, norm1_g, w_in,
           conv_w, conv_b, lru_wa, lru_ba, lru_wx, lru_bx, lru_lambda, hg_lb, hg_norm_g, attn_sink,
           w_branch, w_out, norm2_g, peer_wq, peer_keys, peer_u, peer_v, final_g):
    depth = w_in.shape[0]
    bp, tp, _ = x_prompt.shape
    bs, ts, _ = x_sample.shape
    tpad = SAMPLE_PAD_T
    assert ts <= tpad and tp % WINDOW == 0 and ts >= CONV_W - 1 and cache_k.shape[2] == WINDOW
    n_p = bp * tp
    n_s = bs * tpad

    xs_pad = jnp.pad(x_sample, ((0, 0), (0, tpad - ts), (0, 0)))
    x = jnp.concatenate([x_prompt.reshape(n_p, D_MODEL), xs_pad.reshape(n_s, D_MODEL)], axis=0)

    zeros_conv = jnp.zeros((bp, SUBLANES, MIX_W), F32)
    zeros_h = jnp.zeros((bp, MIX_W), F32)
    zeros_s = jnp.zeros((bp,) + state_hgrn.shape[2:], F32)

    p_st, s_st = [], []
    for l in range(depth):
        lw = dict(conv_w=conv_w[l], conv_b=conv_b[l].reshape(1, MIX_W),
                  wa=lru_wa[l].astype(BF16), ba=lru_ba[l].reshape(1, MIX_W),
                  wx=lru_wx[l].astype(BF16), bx=lru_bx[l].reshape(1, MIX_W),
                  lam=lru_lambda[l].reshape(1, MIX_W))
        proj = _norm_matmul(x, norm1_g[l], w_in[l])

        ya_p, h_p, xl_p = _conv_lru(proj, 0, bp, tp, tp, zeros_conv, zeros_h, lw)
        yb_p, st_p = _hgrn(proj, 0, bp, tp, tp, zeros_s, hg_lb, hg_norm_g[l], l)
        yc_p, kn_p, vn_p = _swa(proj, 0, bp, tp, attn_sink[l])
        conv0 = jnp.pad(state_conv[l], ((0, 0), (SUBLANES - (CONV_W - 1), 0), (0, 0)))
        ya_s, h_s, xl_s = _conv_lru(proj, n_p, bs, tpad, ts, conv0, state_lru[l], lw)
        yb_s, st_s = _hgrn(proj, n_p, bs, tpad, ts, state_hgrn[l], hg_lb, hg_norm_g[l], l)
        yc_s, kn_s, vn_s = _swa(
            proj, n_p, bs, tpad, attn_sink[l],
            cache=(cache_k[l].reshape(bs * WINDOW, KV_W), cache_v[l].reshape(bs * WINDOW, KV_W)))

        z = _branch_mix((ya_p, yb_p, yc_p), (ya_s, yb_s, yc_s), proj, w_branch[l].astype(BF16))
        x = _out_proj(z, w_out[l].astype(BF16), x)

        uv = _pack_uv(peer_u[l], peer_v[l])
        keys = peer_keys[l].reshape(2 * PEER_HEADS, N_KEYS, PEER_DKEY).astype(BF16)
        last = l == depth - 1
        splits = [(0, n_p), (n_p, n_s)] if last else [(0, n_p + n_s)]
        outs = _peer(x, norm2_g[l], peer_wq[l].astype(BF16), keys, uv, final_g, last, splits)
        x = outs[0]

        p_st.append((xl_p[:, SUBLANES - (CONV_W - 1):], h_p, st_p,
                     kn_p.reshape(bp, WINDOW, KV_HEADS, HEAD_DIM),
                     vn_p.reshape(bp, WINDOW, KV_HEADS, HEAD_DIM)))
        k_new = kn_s[:, :ts].reshape(bs, ts, KV_HEADS, HEAD_DIM)
        v_new = vn_s[:, :ts].reshape(bs, ts, KV_HEADS, HEAD_DIM)
        s_st.append((xl_s[:, ts - (CONV_W - 1):ts], h_s, st_s,
                     jnp.concatenate([cache_k[l][:, ts:], k_new], axis=1),
                     jnp.concatenate([cache_v[l][:, ts:], v_new], axis=1)))

    y_prompt = outs[0].reshape(bp, tp, D_MODEL)
    y_sample = outs[1].reshape(bs, tpad, D_MODEL)[:, :ts]
    stack = lambda sts, i: jnp.stack([s[i] for s in sts])
    return (y_prompt, y_sample,
            stack(p_st, 0), stack(p_st, 1), stack(p_st, 2), stack(p_st, 3), stack(p_st, 4),
            stack(s_st, 0), stack(s_st, 1), stack(s_st, 2), stack(s_st, 3), stack(s_st, 4))
```

```python
import functools

import jax
import jax.numpy as jnp
from jax import lax
from jax.experimental import pallas as pl
from jax.experimental.pallas import tpu as pltpu

F32 = jnp.float32
BF16 = jnp.bfloat16

D_MODEL = 2048
MIX_W = D_MODEL // 2
N_BRANCH = 3
LRU_BLOCKS = 8
LRU_BD = MIX_W // LRU_BLOCKS
CONV_W = 4
LRU_C = 8.0
HG_HEADS = 8
HG_DK = MIX_W // HG_HEADS
ATT_HEADS = 16
HEAD_DIM = MIX_W // ATT_HEADS
KV_HEADS = 4
GROUP = ATT_HEADS // KV_HEADS
KV_W = KV_HEADS * HEAD_DIM
WINDOW = 128
ATT_SCALE = HEAD_DIM ** -0.5
MASK_VALUE = -1e30
PEER_HEADS = 8
N_KEYS = 128
PEER_TOPK = 16
PEER_DKEY = 128
PEER_SLOTS = PEER_HEADS * PEER_TOPK
EPS = 1e-6

COL_XA = 0
COL_HQ = MIX_W
COL_HF = 2 * MIX_W
COL_HI = 3 * MIX_W
COL_HG = 4 * MIX_W
COL_AQ = 5 * MIX_W
COL_AK = 6 * MIX_W
COL_AV = COL_AK + KV_W
COL_GATE = COL_AV + KV_W
IN_COLS = COL_GATE + N_BRANCH * D_MODEL

SUBLANES = 8
SAMPLE_PAD_T = SUBLANES
VMEM_LIMIT = 48 * 1024 * 1024


def _pick(n, candidates):
    for c in candidates:
        if n % c == 0:
            return c
    raise ValueError(f"no tile in {candidates} divides {n}")


def _cparams(sem, vmem=VMEM_LIMIT):
    return pltpu.CompilerParams(dimension_semantics=sem, vmem_limit_bytes=vmem)


def _rms(x, g):
    return x * lax.rsqrt(jnp.mean(x * x, axis=-1, keepdims=True) + EPS) * g


def _dot_nt(a, b):
    return lax.dot_general(a, b, (((1,), (1,)), ((), ())), preferred_element_type=F32)


def _dot_tn(a, b):
    return lax.dot_general(a, b, (((0,), (0,)), ((), ())), preferred_element_type=F32)


def _norm_matmul_kernel(x_ref, g_ref, w_ref, o_ref, xn_ref):
    @pl.when(pl.program_id(1) == 0)
    def _():
        xn_ref[...] = _rms(x_ref[...], g_ref[...]).astype(BF16)

    o_ref[...] = jnp.dot(xn_ref[...], w_ref[...].astype(BF16), preferred_element_type=F32)


def _norm_matmul(x, g, w_all, layer):
    n, k = x.shape
    c = w_all.shape[2]
    tm = _pick(n, (1024, 512, 256, 128, 64))
    tn = _pick(c, (512, 256, 128))
    return pl.pallas_call(
        _norm_matmul_kernel,
        out_shape=jax.ShapeDtypeStruct((n, c), F32),
        grid=(n // tm, c // tn),
        in_specs=[pl.BlockSpec((tm, k), lambda i, j: (i, 0)),
                  pl.BlockSpec((1, k), lambda i, j: (0, 0)),
                  pl.BlockSpec((None, k, tn), lambda i, j: (layer, 0, j))],
        out_specs=pl.BlockSpec((tm, tn), lambda i, j: (i, j)),
        scratch_shapes=[pltpu.VMEM((tm, k), BF16)],
        compiler_params=_cparams(("parallel", "arbitrary")),
        name="norm_in_proj",
    )(x, g.reshape(1, k), w_all)


def _lru_kernel(xa_ref, c0_ref, h0_ref, cw_ref, cb_ref, wa_ref, ba_ref, wx_ref, bx_ref, lam_ref,
                ya_ref, hl_ref, xl_ref, tail_ref, h_ref, *, tc, last_row):
    j = pl.program_id(1)

    @pl.when(j == 0)
    def _():
        tail_ref[...] = c0_ref[0]
        h_ref[...] = h0_ref[0]

    x = xa_ref[...]
    xe = jnp.concatenate([tail_ref[...], x], axis=0)
    y = cb_ref[...] + x * cw_ref[CONV_W - 1:CONV_W, :]
    for d in range(1, CONV_W):
        xs = pltpu.roll(xe, d, axis=0)[SUBLANES:SUBLANES + tc]
        y = y + xs * cw_ref[CONV_W - 1 - d:CONV_W - d, :]
    tail_ref[...] = x[tc - SUBLANES:tc]

    y16 = y.astype(BF16)
    ra, rx = [], []
    for n in range(LRU_BLOCKS):
        yb = y16[:, n * LRU_BD:(n + 1) * LRU_BD]
        ra.append(jnp.dot(yb, wa_ref[n], preferred_element_type=F32))
        rx.append(jnp.dot(yb, wx_ref[n], preferred_element_type=F32))
    r = jax.nn.sigmoid(jnp.concatenate(ra, axis=1) + ba_ref[...])
    ig = jax.nn.sigmoid(jnp.concatenate(rx, axis=1) + bx_ref[...])
    nl = -lam_ref[...]
    softplus = jnp.maximum(nl, 0.0) + jnp.log(1.0 + jnp.exp(-jnp.abs(nl)))
    log_a = -LRU_C * r * softplus
    a = jnp.exp(log_a)
    u = jnp.sqrt(jnp.maximum(1.0 - a * a, 0.0)) * (ig * y)

    rows = lax.broadcasted_iota(jnp.int32, (tc, MIX_W), 0)
    d = 1
    while d < tc:
        keep = rows >= d
        a_sh = jnp.where(keep, pltpu.roll(a, d, axis=0), 1.0)
        u_sh = jnp.where(keep, pltpu.roll(u, d, axis=0), 0.0)
        u = a * u_sh + u
        a = a * a_sh
        d *= 2
    h = a * h_ref[...] + u
    ya_ref[...] = h
    h_ref[...] = h[tc - 1:tc]

    @pl.when(j == pl.num_programs(1) - 1)
    def _():
        hl_ref[0] = h[last_row:last_row + 1]
        xl_ref[0] = x[tc - SUBLANES:tc]


def _conv_lru(proj, row0, nseq, t, t_valid, conv0, h0, lw):
    tc = _pick(t, (256, 128, 64, 32, 16, 8))
    nchunk = t // tc
    blk0 = row0 // tc
    last_row = (t_valid - 1) - (nchunk - 1) * tc
    const2 = lambda b, j: (0, 0)
    const3 = lambda b, j: (0, 0, 0)
    kern = functools.partial(_lru_kernel, tc=tc, last_row=last_row)
    ya, hl, xl = pl.pallas_call(
        kern,
        out_shape=(jax.ShapeDtypeStruct((nseq * t, MIX_W), F32),
                   jax.ShapeDtypeStruct((nseq, 1, MIX_W), F32),
                   jax.ShapeDtypeStruct((nseq, SUBLANES, MIX_W), F32)),
        grid=(nseq, nchunk),
        in_specs=[pl.BlockSpec((tc, MIX_W), lambda b, j: (blk0 + b * nchunk + j, COL_XA // MIX_W)),
                  pl.BlockSpec((1, SUBLANES, MIX_W), lambda b, j: (b, 0, 0)),
                  pl.BlockSpec((1, 1, MIX_W), lambda b, j: (b, 0, 0)),
                  pl.BlockSpec((CONV_W, MIX_W), const2),
                  pl.BlockSpec((1, MIX_W), const2),
                  pl.BlockSpec((LRU_BLOCKS, LRU_BD, LRU_BD), const3),
                  pl.BlockSpec((1, MIX_W), const2),
                  pl.BlockSpec((LRU_BLOCKS, LRU_BD, LRU_BD), const3),
                  pl.BlockSpec((1, MIX_W), const2),
                  pl.BlockSpec((1, MIX_W), const2)],
        out_specs=(pl.BlockSpec((tc, MIX_W), lambda b, j: (b * nchunk + j, 0)),
                   pl.BlockSpec((1, 1, MIX_W), lambda b, j: (b, 0, 0)),
                   pl.BlockSpec((1, SUBLANES, MIX_W), lambda b, j: (b, 0, 0))),
        scratch_shapes=[pltpu.VMEM((SUBLANES, MIX_W), F32), pltpu.VMEM((1, MIX_W), F32)],
        compiler_params=_cparams(("parallel", "arbitrary")),
        name="conv_rglru",
    )(proj, conv0, h0.reshape(nseq, 1, MIX_W), lw["conv_w"], lw["conv_b"], lw["wa"], lw["ba"],
      lw["wx"], lw["bx"], lw["lam"])
    return ya, hl.reshape(nseq, MIX_W), xl


def _hgrn_kernel(q_ref, f_ref, i_ref, g_ref, lb_ref, ng_ref, s0_ref, y_ref, st_out_ref, st_ref,
                 *, cc, t_valid, layer):
    c = pl.program_id(1)
    last = pl.num_programs(1) - 1

    @pl.when(c == 0)
    def _():
        for h in range(HG_HEADS):
            st_ref[h] = s0_ref[0, h].T

    lbp = lb_ref[...]
    e = jnp.exp(lbp - jnp.max(lbp, axis=0, keepdims=True))
    sm = e / jnp.sum(e, axis=0, keepdims=True)
    lower = jnp.zeros((1, MIX_W), F32)
    for i in range(1, layer + 1):
        lower = lower + sm[i:i + 1]

    rows = lax.broadcasted_iota(jnp.int32, (cc, HG_DK), 0)
    ti = lax.broadcasted_iota(jnp.int32, (cc, cc), 0)
    si = lax.broadcasted_iota(jnp.int32, (cc, cc), 1)
    tx = jnp.bitwise_xor(ti, si)
    causal = ti > si

    for h in range(HG_HEADS):
        sl = slice(h * HG_DK, (h + 1) * HG_DK)
        lb = lower[:, sl]
        forget = lb + (1.0 - lb) * jax.nn.sigmoid(f_ref[:, sl])
        g = jnp.log(forget)
        kk = 1.0 - forget
        if t_valid < cc:
            g = jnp.where(rows < t_valid, g, 0.0)
            kk = jnp.where(rows < t_valid, kk, 0.0)
        qh = jax.nn.silu(q_ref[:, sl])
        v16 = i_ref[:, sl].astype(BF16)

        b = g
        d = 1
        while d < cc:
            b = b + jnp.where(rows >= d, pltpu.roll(b, d, axis=0), 0.0)
            d *= 2

        scores = jnp.zeros((cc, cc), F32)
        fm = b
        m = 1
        while m < cc:
            odd = jnp.bitwise_and(rows, m) != 0
            em = jnp.where(rows >= m, pltpu.roll(fm, m, axis=0), 0.0)
            ez = jnp.exp(jnp.where(odd, b - em, fm - b))
            sc = _dot_nt((qh * ez).astype(BF16), (kk * ez).astype(BF16))
            scores = jnp.where((tx >= m) & (tx < 2 * m) & causal, sc, scores)
            fm = jnp.where(odd, fm, pltpu.roll(fm, cc - m, axis=0))
            m *= 2
        scores = jnp.where(ti == si, jnp.sum(qh * kk, axis=1, keepdims=True), scores)

        st = st_ref[h]
        b_last = b[cc - 1:cc]
        o = (_dot_nt((qh * jnp.exp(b)).astype(BF16), st.astype(BF16))
             + jnp.dot(scores.astype(BF16), v16, preferred_element_type=F32))
        st_new = st * jnp.exp(b_last) + _dot_tn(v16, (kk * jnp.exp(b_last - b)).astype(BF16))
        st_ref[h] = st_new

        on = _rms(o, ng_ref[:, sl])
        y_ref[:, sl] = on * jax.nn.silu(g_ref[:, sl])

    @pl.when(c == last)
    def _():
        for h in range(HG_HEADS):
            st_out_ref[0, h] = st_ref[h].T


def _hgrn(proj, row0, nseq, t, t_valid, s0_all, s0_layer, hg_lb, norm_g, layer):
    cc = _pick(t, (128, 64, 32, 16, 8))
    nchunk = t // cc
    blk0 = row0 // cc
    depth = hg_lb.shape[0]

    def col(c0):
        return pl.BlockSpec((cc, MIX_W), lambda b, j: (blk0 + b * nchunk + j, c0 // MIX_W))

    kern = functools.partial(_hgrn_kernel, cc=cc, t_valid=min(t_valid, cc) if nchunk == 1 else cc,
                             layer=layer)
    yb, st = pl.pallas_call(
        kern,
        out_shape=(jax.ShapeDtypeStruct((nseq * t, MIX_W), F32),
                   jax.ShapeDtypeStruct(s0_all.shape[1:], F32)),
        grid=(nseq, nchunk),
        in_specs=[col(COL_HQ), col(COL_HF), col(COL_HI), col(COL_HG),
                  pl.BlockSpec((depth, MIX_W), lambda b, j: (0, 0)),
                  pl.BlockSpec((1, MIX_W), lambda b, j: (0, 0)),
                  pl.BlockSpec((None, 1, HG_HEADS, HG_DK, HG_DK), lambda b, j: (s0_layer, b, 0, 0, 0))],
        out_specs=(pl.BlockSpec((cc, MIX_W), lambda b, j: (b * nchunk + j, 0)),
                   pl.BlockSpec((1, HG_HEADS, HG_DK, HG_DK), lambda b, j: (b, 0, 0, 0))),
        scratch_shapes=[pltpu.VMEM((HG_HEADS, HG_DK, HG_DK), F32)],
        compiler_params=_cparams(("parallel", "arbitrary")),
        name="hgrn2",
    )(proj, proj, proj, proj, hg_lb, norm_g.reshape(1, MIX_W), s0_all)
    return yb, st


def _swa_kernel(sink_ref, q_ref, kc_ref, vc_ref, kp_ref, vp_ref, o_ref, kn_ref, vn_ref,
                *, tq, prev_always):
    i = pl.program_id(1)

    @pl.when(i == pl.num_programs(1) - 1)
    def _():
        kn_ref[0] = kc_ref[...]
        vn_ref[0] = vc_ref[...]

    rows = GROUP * tq
    ri = lax.broadcasted_iota(jnp.int32, (rows, WINDOW), 0)
    ci = lax.broadcasted_iota(jnp.int32, (rows, WINDOW), 1)
    grp = ri // tq
    qi = ri - grp * tq
    if prev_always:
        prev_ok = ci > qi
    else:
        prev_ok = ci > qi + jnp.where(i > 0, 0, WINDOW)
    cur_ok = ci <= qi
    grp1 = grp[:, 0:1]

    q = q_ref[...]
    outs = [None] * ATT_HEADS
    for kh in range(KV_HEADS):
        ks = slice(kh * HEAD_DIM, (kh + 1) * HEAD_DIM)
        kp = kp_ref[:, ks].astype(BF16)
        vp = vp_ref[:, ks].astype(BF16)
        kc = kc_ref[:, ks]
        vc = vc_ref[:, ks]
        if tq < WINDOW:
            pad = jnp.zeros((WINDOW - tq, HEAD_DIM), F32)
            kc = jnp.concatenate([kc, pad], axis=0)
            vc = jnp.concatenate([vc, pad], axis=0)
        kc = kc.astype(BF16)
        vc = vc.astype(BF16)
        qs = jnp.concatenate(
            [q[:, (kh * GROUP + g) * HEAD_DIM:(kh * GROUP + g + 1) * HEAD_DIM] for g in range(GROUP)],
            axis=0).astype(BF16)
        sp = jnp.where(prev_ok, _dot_nt(qs, kp) * ATT_SCALE, MASK_VALUE)
        sc = jnp.where(cur_ok, _dot_nt(qs, kc) * ATT_SCALE, MASK_VALUE)
        sk = jnp.zeros((rows, 1), F32)
        for g in range(GROUP):
            sk = jnp.where(grp1 == g, sink_ref[kh * GROUP + g], sk)
        mx = jnp.maximum(jnp.maximum(jnp.max(sp, axis=-1, keepdims=True),
                                     jnp.max(sc, axis=-1, keepdims=True)), sk)
        ep = jnp.exp(sp - mx)
        ec = jnp.exp(sc - mx)
        den = (jnp.sum(ep, axis=-1, keepdims=True) + jnp.sum(ec, axis=-1, keepdims=True)
               + jnp.exp(sk - mx))
        o = (jnp.dot((ep / den).astype(BF16), vp, preferred_element_type=F32)
             + jnp.dot((ec / den).astype(BF16), vc, preferred_element_type=F32))
        for g in range(GROUP):
            outs[kh * GROUP + g] = o[g * tq:(g + 1) * tq]
    o_ref[...] = jnp.concatenate(outs, axis=1)


def _swa(proj, row0, nseq, t, sink, cache=None):
    tq = _pick(t, (WINDOW, SUBLANES))
    nb = t // tq
    blk0 = row0 // tq
    kcol = COL_AK // KV_W
    vcol = COL_AV // KV_W
    cur = lambda c0: pl.BlockSpec((tq, KV_W), lambda b, j: (blk0 + b * nb + j, c0))
    if cache is None:
        kprev, vprev = proj, proj
        prev = lambda c0: pl.BlockSpec(
            (WINDOW, KV_W), lambda b, j: (blk0 + b * nb + jnp.maximum(j - 1, 0), c0))
        prev_specs = [prev(kcol), prev(vcol)]
    else:
        assert nb == 1
        kprev, vprev = cache
        prev_specs = [pl.BlockSpec((WINDOW, KV_W), lambda b, j: (b, 0))] * 2
    kern = functools.partial(_swa_kernel, tq=tq, prev_always=cache is not None)
    new_kv = jax.ShapeDtypeStruct((nseq, tq, KV_W), F32)
    new_spec = pl.BlockSpec((1, tq, KV_W), lambda b, j: (b, 0, 0))
    return pl.pallas_call(
        kern,
        out_shape=(jax.ShapeDtypeStruct((nseq * t, MIX_W), F32), new_kv, new_kv),
        grid=(nseq, nb),
        in_specs=[pl.BlockSpec(memory_space=pltpu.SMEM),
                  pl.BlockSpec((tq, MIX_W), lambda b, j: (blk0 + b * nb + j, COL_AQ // MIX_W)),
                  cur(kcol), cur(vcol)] + prev_specs,
        out_specs=(pl.BlockSpec((tq, MIX_W), lambda b, j: (b * nb + j, 0)), new_spec, new_spec),
        compiler_params=_cparams(("parallel", "arbitrary")),
        name="swa",
    )(sink, proj, proj, proj, kprev, vprev)


def _mix_kernel(ya_p_ref, yb_p_ref, yc_p_ref, ya_s_ref, yb_s_ref, yc_s_ref, g0_ref, g1_ref, g2_ref,
                w_ref, z_ref, *, prompt_tiles):
    is_prompt = pl.program_id(0) < prompt_tiles
    acc = None
    for n, (yp_ref, ys_ref, g_ref) in enumerate(((ya_p_ref, ya_s_ref, g0_ref),
                                                 (yb_p_ref, yb_s_ref, g1_ref),
                                                 (yc_p_ref, yc_s_ref, g2_ref))):
        y = jnp.where(is_prompt, yp_ref[...], ys_ref[...])
        p = jnp.dot(y.astype(BF16), w_ref[n], preferred_element_type=F32)
        p = jax.nn.sigmoid(g_ref[...]) * p
        acc = p if acc is None else acc + p
    z_ref[...] = acc.astype(BF16)


def _branch_mix(y_prompt, y_sample, proj, w_br):
    n_p = y_prompt[0].shape[0]
    n_s = y_sample[0].shape[0]
    tm = _pick(n_s, (512, 256, 128, 64))
    assert n_p % tm == 0
    pt = n_p // tm
    tn = 512
    psp = pl.BlockSpec((tm, MIX_W), lambda i, j: (jnp.minimum(i, pt - 1), 0))
    ssp = pl.BlockSpec((tm, MIX_W), lambda i, j: (jnp.maximum(i - pt, 0), 0))
    gsp = lambda b: pl.BlockSpec((tm, tn), lambda i, j: (i, (COL_GATE + b * D_MODEL) // tn + j))
    return pl.pallas_call(
        functools.partial(_mix_kernel, prompt_tiles=pt),
        out_shape=jax.ShapeDtypeStruct((n_p + n_s, D_MODEL), BF16),
        grid=((n_p + n_s) // tm, D_MODEL // tn),
        in_specs=[psp, psp, psp, ssp, ssp, ssp, gsp(0), gsp(1), gsp(2),
                  pl.BlockSpec((N_BRANCH, MIX_W, tn), lambda i, j: (0, 0, j))],
        out_specs=pl.BlockSpec((tm, tn), lambda i, j: (i, j)),
        compiler_params=_cparams(("parallel", "arbitrary")),
        name="branch_mix",
    )(*y_prompt, *y_sample, proj, proj, proj, w_br)


def _out_kernel(z_ref, w_ref, x_ref, o_ref):
    o_ref[...] = x_ref[...] + jnp.dot(z_ref[...], w_ref[...], preferred_element_type=F32)


def _out_proj(z, w, x):
    n = z.shape[0]
    tm = _pick(n, (1024, 512, 256, 128, 64))
    tn = 512
    return pl.pallas_call(
        _out_kernel,
        out_shape=jax.ShapeDtypeStruct((n, D_MODEL), F32),
        grid=(n // tm, D_MODEL // tn),
        in_specs=[pl.BlockSpec((tm, D_MODEL), lambda i, j: (i, 0)),
                  pl.BlockSpec((D_MODEL, tn), lambda i, j: (0, j)),
                  pl.BlockSpec((tm, tn), lambda i, j: (i, j))],
        out_specs=pl.BlockSpec((tm, tn), lambda i, j: (i, j)),
        compiler_params=_cparams(("parallel", "arbitrary")),
        name="out_proj",
    )(z, w, x)


def _peer_score_kernel(x_ref, g_ref, wq_ref, keys_ref, st_ref):
    xn = _rms(x_ref[...], g_ref[...]).astype(BF16)
    q16 = jnp.dot(xn, wq_ref[...], preferred_element_type=F32).astype(BF16)
    for hc in range(2 * PEER_HEADS):
        sl = slice(hc * PEER_DKEY, (hc + 1) * PEER_DKEY)
        st_ref[hc * N_KEYS:(hc + 1) * N_KEYS, :] = _dot_nt(keys_ref[hc], q16[:, sl])


def _peer_scores(x, g, wq, keys):
    n = x.shape[0]
    tm = _pick(n, (256, 128))
    rows = 2 * PEER_HEADS * N_KEYS
    return pl.pallas_call(
        _peer_score_kernel,
        out_shape=jax.ShapeDtypeStruct((rows, n), F32),
        grid=(n // tm,),
        in_specs=[pl.BlockSpec((tm, D_MODEL), lambda i: (i, 0)),
                  pl.BlockSpec((1, D_MODEL), lambda i: (0, 0)),
                  pl.BlockSpec((D_MODEL, rows), lambda i: (0, 0)),
                  pl.BlockSpec((2 * PEER_HEADS, N_KEYS, PEER_DKEY), lambda i: (0, 0, 0))],
        out_specs=pl.BlockSpec((rows, tm), lambda i: (0, i)),
        compiler_params=_cparams(("parallel",)),
        name="peer_scores",
    )(x, g.reshape(1, D_MODEL), wq, keys)


def _top_rows(s, idx_payload, k):
    r, l = s.shape
    riota = lax.broadcasted_iota(jnp.int32, (r, l), 0).astype(F32)
    kiota = lax.broadcasted_iota(jnp.int32, (k, l), 0)
    vals = jnp.zeros((k, l), F32)
    pay = jnp.zeros((k, l), jnp.int32)
    for j in range(k):
        m = jnp.max(s, axis=0, keepdims=True)
        pos = jnp.min(jnp.where(s == m, riota, float(r)), axis=0, keepdims=True)
        hit = riota == pos
        if idx_payload is None:
            p = pos.astype(jnp.int32)
        else:
            p = jnp.sum(jnp.where(hit, idx_payload, 0), axis=0, keepdims=True)
        s = jnp.where(hit, -jnp.inf, s)
        vals = jnp.where(kiota == j, m, vals)
        pay = jnp.where(kiota == j, p, pay)
    return vals, pay


def _peer_topk_kernel(st_ref, idx_ref, gate_ref):
    for h in range(PEER_HEADS):
        base = h * 2 * N_KEYS
        v0, i0 = _top_rows(st_ref[base:base + N_KEYS, :], None, PEER_TOPK)
        v1, i1 = _top_rows(st_ref[base + N_KEYS:base + 2 * N_KEYS, :], None, PEER_TOPK)
        l = v0.shape[1]
        sub = lax.broadcasted_iota(jnp.int32, (SUBLANES, l), 0)
        cs, ci = [v0[0:1] + v1], [i0[0:1] * N_KEYS + i1]
        for a in range(1, SUBLANES):
            nb = PEER_TOPK // (a + 1)
            cs.append(jnp.where(sub < nb, v0[a:a + 1] + v1[0:SUBLANES], -jnp.inf))
            ci.append(i0[a:a + 1] * N_KEYS + i1[0:SUBLANES])
        cs.append(v0[SUBLANES:PEER_TOPK] + v1[0:1])
        ci.append(i0[SUBLANES:PEER_TOPK] * N_KEYS + i1[0:1])
        sel_s, sel_i = _top_rows(jnp.concatenate(cs, axis=0), jnp.concatenate(ci, axis=0), PEER_TOPK)
        e = jnp.exp(sel_s - jnp.max(sel_s, axis=0, keepdims=True))
        gate = e / jnp.sum(e, axis=0, keepdims=True)
        idx_ref[h * PEER_TOPK:(h + 1) * PEER_TOPK, :] = sel_i
        gate_ref[h * PEER_TOPK:(h + 1) * PEER_TOPK, :] = gate


def _peer_topk(st):
    rows, n = st.shape
    tl = _pick(n, (256, 128))
    return pl.pallas_call(
        _peer_topk_kernel,
        out_shape=(jax.ShapeDtypeStruct((PEER_SLOTS, n), jnp.int32),
                   jax.ShapeDtypeStruct((PEER_SLOTS, n), F32)),
        grid=(n // tl,),
        in_specs=[pl.BlockSpec((rows, tl), lambda i: (0, i))],
        out_specs=(pl.BlockSpec((PEER_SLOTS, tl), lambda i: (0, i)),
                   pl.BlockSpec((PEER_SLOTS, tl), lambda i: (0, i))),
        compiler_params=_cparams(("parallel",)),
        name="peer_topk",
    )(st)


LANES = 128
XROWS = D_MODEL // LANES
SLAB_PITCH = XROWS + 1
PEER_TB = 16
HI_MASK = -65536


def _pack_kernel(u_ref, v_ref, o_ref):
    for j in range(XROWS):
        sl = slice(j * LANES, (j + 1) * LANES)
        ub = lax.bitcast_convert_type(u_ref[:, sl].astype(BF16).astype(F32), jnp.int32)
        vb = lax.bitcast_convert_type(v_ref[:, sl].astype(BF16).astype(F32), jnp.int32)
        o_ref[:, j, :] = jnp.bitwise_or(ub, lax.shift_right_logical(vb, 16))


def _pack_uv(u_all, v_all, layer):
    e = u_all.shape[1]
    te = _pick(e, (256, 128))
    spec = pl.BlockSpec((None, te, D_MODEL), lambda i: (layer, i, 0))
    return pl.pallas_call(
        _pack_kernel,
        out_shape=jax.ShapeDtypeStruct((e, XROWS, LANES), jnp.int32),
        grid=(e // te,),
        in_specs=[spec, spec],
        out_specs=pl.BlockSpec((te, XROWS, LANES), lambda i: (i, 0, 0)),
        compiler_params=_cparams(("parallel",)),
        name="peer_pack",
    )(u_all, v_all)


def _peer_gather_kernel(idx0_ref, idx1_ref, idx2_ref, gate_ref, x_ref, g_ref, fg_ref, uv_ref, o_ref,
                        buf0_ref, buf1_ref, sem_ref, *, tb, final_norm):
    i = pl.program_id(0)
    last = pl.num_programs(0) - 1
    bufs = (buf0_ref, buf1_ref)

    def start_token(idx_ref, t, b):
        for k in range(PEER_SLOTS):
            r = t * PEER_SLOTS + k
            pltpu.async_copy(uv_ref.at[idx_ref[t, k]],
                             bufs[b].at[pl.ds(r * SLAB_PITCH, XROWS), :],
                             sem_ref.at[b], priority=k % 2)

    def wait_buf(b):
        filled = bufs[b].at[pl.ds(0, tb * PEER_SLOTS * XROWS), :]
        pltpu.make_async_copy(filled, filled, sem_ref.at[b]).wait()

    @pl.when(i == 0)
    def _():
        for t in range(tb):
            start_token(idx0_ref, t, 0)

    x = x_ref[...]
    xn = _rms(x, g_ref[...])
    lane = lax.broadcasted_iota(jnp.int32, (PEER_SLOTS, 2 * tb), 1)
    gates = gate_ref[0]
    outs = []
    for half, next_idx_ref in enumerate((idx1_ref, idx2_ref)):
        wait_buf(half)
        for t in range(tb):
            start_token(next_idx_ref, t, 1 - half)
            row = half * tb + t
            base = t * PEER_SLOTS * SLAB_PITCH

            def tile(j):
                return bufs[half][pl.ds(base + j, PEER_SLOTS, stride=SLAB_PITCH), :]

            acc = None
            for j in range(XROWS):
                u = lax.bitcast_convert_type(jnp.bitwise_and(tile(j), HI_MASK), F32)
                p = u * xn[row:row + 1, j * LANES:(j + 1) * LANES]
                acc = p if acc is None else acc + p
            hpre = jnp.sum(acc, axis=-1, keepdims=True)
            act = 0.5 * hpre * (1.0 + lax.erf(hpre * (2.0 ** -0.5)))
            gate_t = jnp.sum(jnp.where(lane == row, gates, 0.0), axis=-1, keepdims=True)
            w = gate_t * act
            rows = []
            for j in range(XROWS):
                v = lax.bitcast_convert_type(jnp.left_shift(tile(j), 16), F32)
                rows.append(jnp.sum(v * w, axis=0, keepdims=True))
            outs.append(jnp.concatenate(rows, axis=1))
    y = x + jnp.concatenate(outs, axis=0)
    if final_norm:
        y = _rms(y, fg_ref[...])
    o_ref[...] = y

    @pl.when(i == last)
    def _():
        wait_buf(0)


def _peer_gather(x, g, idx, gate, uv, final_g, final_norm, row0, nrows):
    tb = gate.shape[2] // 2
    nblk = nrows // tb
    b0 = row0 // tb
    assert nrows % (2 * tb) == 0 and row0 % (2 * tb) == 0
    kern = functools.partial(_peer_gather_kernel, tb=tb, final_norm=final_norm)
    gspec = pl.BlockSpec((1, D_MODEL), lambda i: (0, 0))
    ispec = lambda f: pl.BlockSpec((tb, PEER_SLOTS), lambda i: (b0 + f(i), 0), memory_space=pltpu.SMEM)
    buf = pltpu.VMEM((tb * PEER_SLOTS * SLAB_PITCH, LANES), jnp.int32)
    return pl.pallas_call(
        kern,
        out_shape=jax.ShapeDtypeStruct((nrows, D_MODEL), F32),
        grid=(nblk // 2,),
        in_specs=[ispec(lambda i: 0), ispec(lambda i: 2 * i + 1),
                  ispec(lambda i: jnp.minimum(2 * i + 2, nblk - 1)),
                  pl.BlockSpec((1, PEER_SLOTS, 2 * tb), lambda i: (b0 // 2 + i, 0, 0)),
                  pl.BlockSpec((2 * tb, D_MODEL), lambda i: (b0 // 2 + i, 0)),
                  gspec, gspec,
                  pl.BlockSpec(memory_space=pl.ANY)],
        out_specs=pl.BlockSpec((2 * tb, D_MODEL), lambda i: (i, 0)),
        scratch_shapes=[buf, buf, pltpu.SemaphoreType.DMA((2,))],
        compiler_params=_cparams(("arbitrary",), vmem=56 * 1024 * 1024),
        name="peer_gather",
    )(idx, idx, idx, gate, x, g.reshape(1, D_MODEL), final_g.reshape(1, D_MODEL), uv)


def _peer(x, g, wq, keys, uv, final_g, final_norm, splits):
    n = x.shape[0]
    st = _peer_scores(x, g, wq, keys)
    idx_t, gate_t = _peer_topk(st)
    idx = idx_t.T
    gate = gate_t.reshape(PEER_SLOTS, n // (2 * PEER_TB), 2 * PEER_TB).transpose(1, 0, 2)
    return [_peer_gather(x, g, idx, gate, uv, final_g, final_norm, r0, nr) for r0, nr in splits]


def kernel(x_prompt, x_sample, state_conv, state_lru, state_hgrn, cache_k, cache_v, norm1_g, w_in,
           conv_w, conv_b, lru_wa, lru_ba, lru_wx, lru_bx, lru_lambda, hg_lb, hg_norm_g, attn_sink,
           w_branch, w_out, norm2_g, peer_wq, peer_keys, peer_u, peer_v, final_g):
    depth = w_in.shape[0]
    bp, tp, _ = x_prompt.shape
    bs, ts, _ = x_sample.shape
    tpad = SAMPLE_PAD_T
    assert ts <= tpad and tp % WINDOW == 0 and ts >= CONV_W - 1 and cache_k.shape[2] == WINDOW
    n_p = bp * tp
    n_s = bs * tpad

    xs_pad = jnp.pad(x_sample, ((0, 0), (0, tpad - ts), (0, 0)))
    x = jnp.concatenate([x_prompt.reshape(n_p, D_MODEL), xs_pad.reshape(n_s, D_MODEL)], axis=0)

    zeros_conv = jnp.zeros((bp, SUBLANES, MIX_W), F32)
    zeros_h = jnp.zeros((bp, MIX_W), F32)
    zeros_s = jnp.zeros((1, bp) + state_hgrn.shape[2:], F32)

    p_st, s_st = [], []
    for l in range(depth):
        lw = dict(conv_w=conv_w[l], conv_b=conv_b[l].reshape(1, MIX_W),
                  wa=lru_wa[l].astype(BF16), ba=lru_ba[l].reshape(1, MIX_W),
                  wx=lru_wx[l].astype(BF16), bx=lru_bx[l].reshape(1, MIX_W),
                  lam=lru_lambda[l].reshape(1, MIX_W))
        proj = _norm_matmul(x, norm1_g[l], w_in, l)

        ya_p, h_p, xl_p = _conv_lru(proj, 0, bp, tp, tp, zeros_conv, zeros_h, lw)
        yb_p, st_p = _hgrn(proj, 0, bp, tp, tp, zeros_s, 0, hg_lb, hg_norm_g[l], l)
        yc_p, kn_p, vn_p = _swa(proj, 0, bp, tp, attn_sink[l])
        conv0 = jnp.pad(state_conv[l], ((0, 0), (SUBLANES - (CONV_W - 1), 0), (0, 0)))
        ya_s, h_s, xl_s = _conv_lru(proj, n_p, bs, tpad, ts, conv0, state_lru[l], lw)
        yb_s, st_s = _hgrn(proj, n_p, bs, tpad, ts, state_hgrn, l, hg_lb, hg_norm_g[l], l)
        yc_s, kn_s, vn_s = _swa(
            proj, n_p, bs, tpad, attn_sink[l],
            cache=(cache_k[l].reshape(bs * WINDOW, KV_W), cache_v[l].reshape(bs * WINDOW, KV_W)))

        z = _branch_mix((ya_p, yb_p, yc_p), (ya_s, yb_s, yc_s), proj, w_branch[l].astype(BF16))
        x = _out_proj(z, w_out[l].astype(BF16), x)

        uv = _pack_uv(peer_u, peer_v, l)
        keys = peer_keys[l].reshape(2 * PEER_HEADS, N_KEYS, PEER_DKEY).astype(BF16)
        last = l == depth - 1
        splits = [(0, n_p), (n_p, n_s)] if last else [(0, n_p + n_s)]
        outs = _peer(x, norm2_g[l], peer_wq[l].astype(BF16), keys, uv, final_g, last, splits)
        x = outs[0]

        p_st.append((xl_p[:, SUBLANES - (CONV_W - 1):], h_p, st_p,
                     kn_p.reshape(bp, WINDOW, KV_HEADS, HEAD_DIM),
                     vn_p.reshape(bp, WINDOW, KV_HEADS, HEAD_DIM)))
        k_new = kn_s[:, :ts].reshape(bs, ts, KV_HEADS, HEAD_DIM)
        v_new = vn_s[:, :ts].reshape(bs, ts, KV_HEADS, HEAD_DIM)
        s_st.append((xl_s[:, ts - (CONV_W - 1):ts], h_s, st_s,
                     jnp.concatenate([cache_k[l][:, ts:], k_new], axis=1),
                     jnp.concatenate([cache_v[l][:, ts:], v_new], axis=1)))

    y_prompt = outs[0].reshape(bp, tp, D_MODEL)
    y_sample = outs[1].reshape(bs, tpad, D_MODEL)[:, :ts]
    stack = lambda sts, i: jnp.stack([s[i] for s in sts])
    return (y_prompt, y_sample,
            stack(p_st, 0), stack(p_st, 1), stack(p_st, 2), stack(p_st, 3), stack(p_st, 4),
            stack(s_st, 0), stack(s_st, 1), stack(s_st, 2), stack(s_st, 3), stack(s_st, 4))
```

```python
import functools

import jax
import jax.numpy as jnp
from jax import lax
from jax.experimental import pallas as pl
from jax.experimental.pallas import tpu as pltpu

F32 = jnp.float32
BF16 = jnp.bfloat16

D_MODEL = 2048
MIX_W = D_MODEL // 2
N_BRANCH = 3
LRU_BLOCKS = 8
LRU_BD = MIX_W // LRU_BLOCKS
CONV_W = 4
LRU_C = 8.0
HG_HEADS = 8
HG_DK = MIX_W // HG_HEADS
ATT_HEADS = 16
HEAD_DIM = MIX_W // ATT_HEADS
KV_HEADS = 4
GROUP = ATT_HEADS // KV_HEADS
KV_W = KV_HEADS * HEAD_DIM
WINDOW = 128
ATT_SCALE = HEAD_DIM ** -0.5
MASK_VALUE = -1e30
PEER_HEADS = 8
N_KEYS = 128
PEER_TOPK = 16
PEER_DKEY = 128
PEER_SLOTS = PEER_HEADS * PEER_TOPK
EPS = 1e-6

COL_XA = 0
COL_HQ = MIX_W
COL_HF = 2 * MIX_W
COL_HI = 3 * MIX_W
COL_HG = 4 * MIX_W
COL_AQ = 5 * MIX_W
COL_AK = 6 * MIX_W
COL_AV = COL_AK + KV_W
COL_GATE = COL_AV + KV_W
IN_COLS = COL_GATE + N_BRANCH * D_MODEL

SUBLANES = 8
SAMPLE_PAD_T = SUBLANES
VMEM_LIMIT = 48 * 1024 * 1024


def _pick(n, candidates):
    for c in candidates:
        if n % c == 0:
            return c
    raise ValueError(f"no tile in {candidates} divides {n}")


def _cparams(sem, vmem=VMEM_LIMIT):
    return pltpu.CompilerParams(dimension_semantics=sem, vmem_limit_bytes=vmem)


def _rms(x, g):
    return x * lax.rsqrt(jnp.mean(x * x, axis=-1, keepdims=True) + EPS) * g


def _dot_nt(a, b):
    return lax.dot_general(a, b, (((1,), (1,)), ((), ())), preferred_element_type=F32)


def _dot_tn(a, b):
    return lax.dot_general(a, b, (((0,), (0,)), ((), ())), preferred_element_type=F32)


def _norm_matmul_kernel(x_ref, g_ref, w_ref, o_ref, xn_ref):
    @pl.when(pl.program_id(1) == 0)
    def _():
        xn_ref[...] = _rms(x_ref[...], g_ref[...]).astype(BF16)

    o_ref[...] = jnp.dot(xn_ref[...], w_ref[...], preferred_element_type=F32)


def _cast_kernel(w_ref, o_ref):
    o_ref[...] = w_ref[...].astype(BF16)


def _layer_bf16(w_all, layer):
    _, k, c = w_all.shape
    tk = _pick(k, (128, 64))
    return pl.pallas_call(
        _cast_kernel,
        out_shape=jax.ShapeDtypeStruct((k, c), BF16),
        grid=(k // tk,),
        in_specs=[pl.BlockSpec((None, tk, c), lambda i: (layer, i, 0))],
        out_specs=pl.BlockSpec((tk, c), lambda i: (i, 0)),
        compiler_params=_cparams(("parallel",)),
        name="cast_bf16",
    )(w_all)


def _norm_matmul(x, g, w):
    n, k = x.shape
    c = w.shape[1]
    tm = _pick(n, (1024, 512, 256, 128, 64))
    tn = _pick(c, (1280, 512, 256, 128))
    return pl.pallas_call(
        _norm_matmul_kernel,
        out_shape=jax.ShapeDtypeStruct((n, c), F32),
        grid=(n // tm, c // tn),
        in_specs=[pl.BlockSpec((tm, k), lambda i, j: (i, 0)),
                  pl.BlockSpec((1, k), lambda i, j: (0, 0)),
                  pl.BlockSpec((k, tn), lambda i, j: (0, j))],
        out_specs=pl.BlockSpec((tm, tn), lambda i, j: (i, j)),
        scratch_shapes=[pltpu.VMEM((tm, k), BF16)],
        compiler_params=_cparams(("parallel", "arbitrary"), vmem=56 * 1024 * 1024),
        name="norm_in_proj",
    )(x, g.reshape(1, k), w)


def _lru_kernel(xa_ref, c0_ref, h0_ref, cw_ref, cb_ref, wa_ref, ba_ref, wx_ref, bx_ref, lam_ref,
                ya_ref, hl_ref, xl_ref, tail_ref, h_ref, *, tc, last_row):
    j = pl.program_id(1)

    @pl.when(j == 0)
    def _():
        tail_ref[...] = c0_ref[0]
        h_ref[...] = h0_ref[0]

    x = xa_ref[...]
    xe = jnp.concatenate([tail_ref[...], x], axis=0)
    y = cb_ref[...] + x * cw_ref[CONV_W - 1:CONV_W, :]
    for d in range(1, CONV_W):
        xs = pltpu.roll(xe, d, axis=0)[SUBLANES:SUBLANES + tc]
        y = y + xs * cw_ref[CONV_W - 1 - d:CONV_W - d, :]
    tail_ref[...] = x[tc - SUBLANES:tc]

    y16 = y.astype(BF16)
    ra, rx = [], []
    for n in range(LRU_BLOCKS):
        yb = y16[:, n * LRU_BD:(n + 1) * LRU_BD]
        ra.append(jnp.dot(yb, wa_ref[n], preferred_element_type=F32))
        rx.append(jnp.dot(yb, wx_ref[n], preferred_element_type=F32))
    r = jax.nn.sigmoid(jnp.concatenate(ra, axis=1) + ba_ref[...])
    ig = jax.nn.sigmoid(jnp.concatenate(rx, axis=1) + bx_ref[...])
    nl = -lam_ref[...]
    softplus = jnp.maximum(nl, 0.0) + jnp.log(1.0 + jnp.exp(-jnp.abs(nl)))
    log_a = -LRU_C * r * softplus
    a = jnp.exp(log_a)
    u = jnp.sqrt(jnp.maximum(1.0 - a * a, 0.0)) * (ig * y)

    rows = lax.broadcasted_iota(jnp.int32, (tc, MIX_W), 0)
    d = 1
    while d < tc:
        keep = rows >= d
        a_sh = jnp.where(keep, pltpu.roll(a, d, axis=0), 1.0)
        u_sh = jnp.where(keep, pltpu.roll(u, d, axis=0), 0.0)
        u = a * u_sh + u
        a = a * a_sh
        d *= 2
    h = a * h_ref[...] + u
    ya_ref[...] = h
    h_ref[...] = h[tc - 1:tc]

    @pl.when(j == pl.num_programs(1) - 1)
    def _():
        hl_ref[0] = h[last_row:last_row + 1]
        xl_ref[0] = x[tc - SUBLANES:tc]


def _conv_lru(proj, row0, nseq, t, t_valid, conv0, h0, lw):
    tc = _pick(t, (256, 128, 64, 32, 16, 8))
    nchunk = t // tc
    blk0 = row0 // tc
    last_row = (t_valid - 1) - (nchunk - 1) * tc
    const2 = lambda b, j: (0, 0)
    const3 = lambda b, j: (0, 0, 0)
    kern = functools.partial(_lru_kernel, tc=tc, last_row=last_row)
    ya, hl, xl = pl.pallas_call(
        kern,
        out_shape=(jax.ShapeDtypeStruct((nseq * t, MIX_W), F32),
                   jax.ShapeDtypeStruct((nseq, 1, MIX_W), F32),
                   jax.ShapeDtypeStruct((nseq, SUBLANES, MIX_W), F32)),
        grid=(nseq, nchunk),
        in_specs=[pl.BlockSpec((tc, MIX_W), lambda b, j: (blk0 + b * nchunk + j, COL_XA // MIX_W)),
                  pl.BlockSpec((1, SUBLANES, MIX_W), lambda b, j: (b, 0, 0)),
                  pl.BlockSpec((1, 1, MIX_W), lambda b, j: (b, 0, 0)),
                  pl.BlockSpec((CONV_W, MIX_W), const2),
                  pl.BlockSpec((1, MIX_W), const2),
                  pl.BlockSpec((LRU_BLOCKS, LRU_BD, LRU_BD), const3),
                  pl.BlockSpec((1, MIX_W), const2),
                  pl.BlockSpec((LRU_BLOCKS, LRU_BD, LRU_BD), const3),
                  pl.BlockSpec((1, MIX_W), const2),
                  pl.BlockSpec((1, MIX_W), const2)],
        out_specs=(pl.BlockSpec((tc, MIX_W), lambda b, j: (b * nchunk + j, 0)),
                   pl.BlockSpec((1, 1, MIX_W), lambda b, j: (b, 0, 0)),
                   pl.BlockSpec((1, SUBLANES, MIX_W), lambda b, j: (b, 0, 0))),
        scratch_shapes=[pltpu.VMEM((SUBLANES, MIX_W), F32), pltpu.VMEM((1, MIX_W), F32)],
        compiler_params=_cparams(("parallel", "arbitrary")),
        name="conv_rglru",
    )(proj, conv0, h0.reshape(nseq, 1, MIX_W), lw["conv_w"], lw["conv_b"], lw["wa"], lw["ba"],
      lw["wx"], lw["bx"], lw["lam"])
    return ya, hl.reshape(nseq, MIX_W), xl


def _hgrn_kernel(q_ref, f_ref, i_ref, g_ref, lb_ref, ng_ref, s0_ref, y_ref, st_out_ref, st_ref,
                 *, cc, t_valid, layer):
    c = pl.program_id(1)
    last = pl.num_programs(1) - 1

    @pl.when(c == 0)
    def _():
        for h in range(HG_HEADS):
            st_ref[h] = s0_ref[0, h].T

    lbp = lb_ref[...]
    e = jnp.exp(lbp - jnp.max(lbp, axis=0, keepdims=True))
    sm = e / jnp.sum(e, axis=0, keepdims=True)
    lower = jnp.zeros((1, MIX_W), F32)
    for i in range(1, layer + 1):
        lower = lower + sm[i:i + 1]

    rows = lax.broadcasted_iota(jnp.int32, (cc, HG_DK), 0)
    ti = lax.broadcasted_iota(jnp.int32, (cc, cc), 0)
    si = lax.broadcasted_iota(jnp.int32, (cc, cc), 1)
    tx = jnp.bitwise_xor(ti, si)
    causal = ti > si

    for h in range(HG_HEADS):
        sl = slice(h * HG_DK, (h + 1) * HG_DK)
        lb = lower[:, sl]
        forget = lb + (1.0 - lb) * jax.nn.sigmoid(f_ref[:, sl])
        g = jnp.log(forget)
        kk = 1.0 - forget
        if t_valid < cc:
            g = jnp.where(rows < t_valid, g, 0.0)
            kk = jnp.where(rows < t_valid, kk, 0.0)
        qh = jax.nn.silu(q_ref[:, sl])
        v16 = i_ref[:, sl].astype(BF16)

        b = g
        d = 1
        while d < cc:
            b = b + jnp.where(rows >= d, pltpu.roll(b, d, axis=0), 0.0)
            d *= 2

        scores = jnp.zeros((cc, cc), F32)
        fm = b
        m = 1
        while m < cc:
            odd = jnp.bitwise_and(rows, m) != 0
            em = jnp.where(rows >= m, pltpu.roll(fm, m, axis=0), 0.0)
            ez = jnp.exp(jnp.where(odd, b - em, fm - b))
            sc = _dot_nt((qh * ez).astype(BF16), (kk * ez).astype(BF16))
            scores = jnp.where((tx >= m) & (tx < 2 * m) & causal, sc, scores)
            fm = jnp.where(odd, fm, pltpu.roll(fm, cc - m, axis=0))
            m *= 2
        scores = jnp.where(ti == si, jnp.sum(qh * kk, axis=1, keepdims=True), scores)

        st = st_ref[h]
        b_last = b[cc - 1:cc]
        o = (_dot_nt((qh * jnp.exp(b)).astype(BF16), st.astype(BF16))
             + jnp.dot(scores.astype(BF16), v16, preferred_element_type=F32))
        st_new = st * jnp.exp(b_last) + _dot_tn(v16, (kk * jnp.exp(b_last - b)).astype(BF16))
        st_ref[h] = st_new

        on = _rms(o, ng_ref[:, sl])
        y_ref[:, sl] = on * jax.nn.silu(g_ref[:, sl])

    @pl.when(c == last)
    def _():
        for h in range(HG_HEADS):
            st_out_ref[0, h] = st_ref[h].T


def _hgrn(proj, row0, nseq, t, t_valid, s0_all, s0_layer, hg_lb, norm_g, layer):
    cc = _pick(t, (128, 64, 32, 16, 8))
    nchunk = t // cc
    blk0 = row0 // cc
    depth = hg_lb.shape[0]

    def col(c0):
        return pl.BlockSpec((cc, MIX_W), lambda b, j: (blk0 + b * nchunk + j, c0 // MIX_W))

    kern = functools.partial(_hgrn_kernel, cc=cc, t_valid=min(t_valid, cc) if nchunk == 1 else cc,
                             layer=layer)
    yb, st = pl.pallas_call(
        kern,
        out_shape=(jax.ShapeDtypeStruct((nseq * t, MIX_W), F32),
                   jax.ShapeDtypeStruct(s0_all.shape[1:], F32)),
        grid=(nseq, nchunk),
        in_specs=[col(COL_HQ), col(COL_HF), col(COL_HI), col(COL_HG),
                  pl.BlockSpec((depth, MIX_W), lambda b, j: (0, 0)),
                  pl.BlockSpec((1, MIX_W), lambda b, j: (0, 0)),
                  pl.BlockSpec((None, 1, HG_HEADS, HG_DK, HG_DK), lambda b, j: (s0_layer, b, 0, 0, 0))],
        out_specs=(pl.BlockSpec((cc, MIX_W), lambda b, j: (b * nchunk + j, 0)),
                   pl.BlockSpec((1, HG_HEADS, HG_DK, HG_DK), lambda b, j: (b, 0, 0, 0))),
        scratch_shapes=[pltpu.VMEM((HG_HEADS, HG_DK, HG_DK), F32)],
        compiler_params=_cparams(("parallel", "arbitrary")),
        name="hgrn2",
    )(proj, proj, proj, proj, hg_lb, norm_g.reshape(1, MIX_W), s0_all)
    return yb, st


def _swa_kernel(sink_ref, q_ref, kc_ref, vc_ref, kp_ref, vp_ref, o_ref, kn_ref, vn_ref,
                *, tq, prev_always):
    i = pl.program_id(1)

    @pl.when(i == pl.num_programs(1) - 1)
    def _():
        kn_ref[0] = kc_ref[...]
        vn_ref[0] = vc_ref[...]

    rows = GROUP * tq
    ri = lax.broadcasted_iota(jnp.int32, (rows, WINDOW), 0)
    ci = lax.broadcasted_iota(jnp.int32, (rows, WINDOW), 1)
    grp = ri // tq
    qi = ri - grp * tq
    if prev_always:
        prev_ok = ci > qi
    else:
        prev_ok = ci > qi + jnp.where(i > 0, 0, WINDOW)
    cur_ok = ci <= qi
    grp1 = grp[:, 0:1]

    q = q_ref[...]
    outs = [None] * ATT_HEADS
    for kh in range(KV_HEADS):
        ks = slice(kh * HEAD_DIM, (kh + 1) * HEAD_DIM)
        kp = kp_ref[:, ks].astype(BF16)
        vp = vp_ref[:, ks].astype(BF16)
        kc = kc_ref[:, ks]
        vc = vc_ref[:, ks]
        if tq < WINDOW:
            pad = jnp.zeros((WINDOW - tq, HEAD_DIM), F32)
            kc = jnp.concatenate([kc, pad], axis=0)
            vc = jnp.concatenate([vc, pad], axis=0)
        kc = kc.astype(BF16)
        vc = vc.astype(BF16)
        qs = jnp.concatenate(
            [q[:, (kh * GROUP + g) * HEAD_DIM:(kh * GROUP + g + 1) * HEAD_DIM] for g in range(GROUP)],
            axis=0).astype(BF16)
        sp = jnp.where(prev_ok, _dot_nt(qs, kp) * ATT_SCALE, MASK_VALUE)
        sc = jnp.where(cur_ok, _dot_nt(qs, kc) * ATT_SCALE, MASK_VALUE)
        sk = jnp.zeros((rows, 1), F32)
        for g in range(GROUP):
            sk = jnp.where(grp1 == g, sink_ref[kh * GROUP + g], sk)
        mx = jnp.maximum(jnp.maximum(jnp.max(sp, axis=-1, keepdims=True),
                                     jnp.max(sc, axis=-1, keepdims=True)), sk)
        ep = jnp.exp(sp - mx)
        ec = jnp.exp(sc - mx)
        den = (jnp.sum(ep, axis=-1, keepdims=True) + jnp.sum(ec, axis=-1, keepdims=True)
               + jnp.exp(sk - mx))
        o = (jnp.dot((ep / den).astype(BF16), vp, preferred_element_type=F32)
             + jnp.dot((ec / den).astype(BF16), vc, preferred_element_type=F32))
        for g in range(GROUP):
            outs[kh * GROUP + g] = o[g * tq:(g + 1) * tq]
    o_ref[...] = jnp.concatenate(outs, axis=1)


def _swa(proj, row0, nseq, t, sink, cache=None):
    tq = _pick(t, (WINDOW, SUBLANES))
    nb = t // tq
    blk0 = row0 // tq
    kcol = COL_AK // KV_W
    vcol = COL_AV // KV_W
    cur = lambda c0: pl.BlockSpec((tq, KV_W), lambda b, j: (blk0 + b * nb + j, c0))
    if cache is None:
        kprev, vprev = proj, proj
        prev = lambda c0: pl.BlockSpec(
            (WINDOW, KV_W), lambda b, j: (blk0 + b * nb + jnp.maximum(j - 1, 0), c0))
        prev_specs = [prev(kcol), prev(vcol)]
    else:
        assert nb == 1
        kprev, vprev = cache
        prev_specs = [pl.BlockSpec((WINDOW, KV_W), lambda b, j: (b, 0))] * 2
    kern = functools.partial(_swa_kernel, tq=tq, prev_always=cache is not None)
    new_kv = jax.ShapeDtypeStruct((nseq, tq, KV_W), F32)
    new_spec = pl.BlockSpec((1, tq, KV_W), lambda b, j: (b, 0, 0))
    return pl.pallas_call(
        kern,
        out_shape=(jax.ShapeDtypeStruct((nseq * t, MIX_W), F32), new_kv, new_kv),
        grid=(nseq, nb),
        in_specs=[pl.BlockSpec(memory_space=pltpu.SMEM),
                  pl.BlockSpec((tq, MIX_W), lambda b, j: (blk0 + b * nb + j, COL_AQ // MIX_W)),
                  cur(kcol), cur(vcol)] + prev_specs,
        out_specs=(pl.BlockSpec((tq, MIX_W), lambda b, j: (b * nb + j, 0)), new_spec, new_spec),
        compiler_params=_cparams(("parallel", "arbitrary")),
        name="swa",
    )(sink, proj, proj, proj, kprev, vprev)


def _mix_kernel(ya_p_ref, yb_p_ref, yc_p_ref, ya_s_ref, yb_s_ref, yc_s_ref, g0_ref, g1_ref, g2_ref,
                w_ref, z_ref, *, prompt_tiles):
    is_prompt = pl.program_id(0) < prompt_tiles
    acc = None
    for n, (yp_ref, ys_ref, g_ref) in enumerate(((ya_p_ref, ya_s_ref, g0_ref),
                                                 (yb_p_ref, yb_s_ref, g1_ref),
                                                 (yc_p_ref, yc_s_ref, g2_ref))):
        y = jnp.where(is_prompt, yp_ref[...], ys_ref[...])
        p = jnp.dot(y.astype(BF16), w_ref[n], preferred_element_type=F32)
        p = jax.nn.sigmoid(g_ref[...]) * p
        acc = p if acc is None else acc + p
    z_ref[...] = acc.astype(BF16)


def _branch_mix(y_prompt, y_sample, proj, w_br):
    n_p = y_prompt[0].shape[0]
    n_s = y_sample[0].shape[0]
    tm = _pick(n_s, (512, 256, 128, 64))
    assert n_p % tm == 0
    pt = n_p // tm
    tn = 512
    psp = pl.BlockSpec((tm, MIX_W), lambda i, j: (jnp.minimum(i, pt - 1), 0))
    ssp = pl.BlockSpec((tm, MIX_W), lambda i, j: (jnp.maximum(i - pt, 0), 0))
    gsp = lambda b: pl.BlockSpec((tm, tn), lambda i, j: (i, (COL_GATE + b * D_MODEL) // tn + j))
    return pl.pallas_call(
        functools.partial(_mix_kernel, prompt_tiles=pt),
        out_shape=jax.ShapeDtypeStruct((n_p + n_s, D_MODEL), BF16),
        grid=((n_p + n_s) // tm, D_MODEL // tn),
        in_specs=[psp, psp, psp, ssp, ssp, ssp, gsp(0), gsp(1), gsp(2),
                  pl.BlockSpec((N_BRANCH, MIX_W, tn), lambda i, j: (0, 0, j))],
        out_specs=pl.BlockSpec((tm, tn), lambda i, j: (i, j)),
        compiler_params=_cparams(("parallel", "arbitrary")),
        name="branch_mix",
    )(*y_prompt, *y_sample, proj, proj, proj, w_br)


def _out_kernel(z_ref, w_ref, x_ref, o_ref):
    o_ref[...] = x_ref[...] + jnp.dot(z_ref[...], w_ref[...], preferred_element_type=F32)


def _out_proj(z, w, x):
    n = z.shape[0]
    tm = _pick(n, (1024, 512, 256, 128, 64))
    tn = 512
    return pl.pallas_call(
        _out_kernel,
        out_shape=jax.ShapeDtypeStruct((n, D_MODEL), F32),
        grid=(n // tm, D_MODEL // tn),
        in_specs=[pl.BlockSpec((tm, D_MODEL), lambda i, j: (i, 0)),
                  pl.BlockSpec((D_MODEL, tn), lambda i, j: (0, j)),
                  pl.BlockSpec((tm, tn), lambda i, j: (i, j))],
        out_specs=pl.BlockSpec((tm, tn), lambda i, j: (i, j)),
        compiler_params=_cparams(("parallel", "arbitrary")),
        name="out_proj",
    )(z, w, x)


def _peer_score_kernel(x_ref, g_ref, wq_ref, keys_ref, st_ref):
    xn = _rms(x_ref[...], g_ref[...]).astype(BF16)
    q16 = jnp.dot(xn, wq_ref[...], preferred_element_type=F32).astype(BF16)
    for hc in range(2 * PEER_HEADS):
        sl = slice(hc * PEER_DKEY, (hc + 1) * PEER_DKEY)
        st_ref[hc * N_KEYS:(hc + 1) * N_KEYS, :] = _dot_nt(keys_ref[hc], q16[:, sl])


def _top_rows(s, idx_payload, k):
    r, l = s.shape
    riota = lax.broadcasted_iota(jnp.int32, (r, l), 0).astype(F32)
    kiota = lax.broadcasted_iota(jnp.int32, (k, l), 0)
    vals = jnp.zeros((k, l), F32)
    pay = jnp.zeros((k, l), jnp.int32)
    for j in range(k):
        m = jnp.max(s, axis=0, keepdims=True)
        pos = jnp.min(jnp.where(s == m, riota, float(r)), axis=0, keepdims=True)
        hit = riota == pos
        if idx_payload is None:
            p = pos.astype(jnp.int32)
        else:
            p = jnp.sum(jnp.where(hit, idx_payload, 0), axis=0, keepdims=True)
        s = jnp.where(hit, -jnp.inf, s)
        vals = jnp.where(kiota == j, m, vals)
        pay = jnp.where(kiota == j, p, pay)
    return vals, pay


def _peer_topk_kernel(st_ref, idx_ref, gate_ref):
    for h in range(PEER_HEADS):
        base = h * 2 * N_KEYS
        v0, i0 = _top_rows(st_ref[base:base + N_KEYS, :], None, PEER_TOPK)
        v1, i1 = _top_rows(st_ref[base + N_KEYS:base + 2 * N_KEYS, :], None, PEER_TOPK)
        l = v0.shape[1]
        sub = lax.broadcasted_iota(jnp.int32, (SUBLANES, l), 0)
        cs, ci = [v0[0:1] + v1], [i0[0:1] * N_KEYS + i1]
        for a in range(1, SUBLANES):
            nb = PEER_TOPK // (a + 1)
            cs.append(jnp.where(sub < nb, v0[a:a + 1] + v1[0:SUBLANES], -jnp.inf))
            ci.append(i0[a:a + 1] * N_KEYS + i1[0:SUBLANES])
        cs.append(v0[SUBLANES:PEER_TOPK] + v1[0:1])
        ci.append(i0[SUBLANES:PEER_TOPK] * N_KEYS + i1[0:1])
        sel_s, sel_i = _top_rows(jnp.concatenate(cs, axis=0), jnp.concatenate(ci, axis=0), PEER_TOPK)
        e = jnp.exp(sel_s - jnp.max(sel_s, axis=0, keepdims=True))
        gate = e / jnp.sum(e, axis=0, keepdims=True)
        idx_ref[h * PEER_TOPK:(h + 1) * PEER_TOPK, :] = sel_i
        gate_ref[h * PEER_TOPK:(h + 1) * PEER_TOPK, :] = gate


def _peer_select_kernel(x_ref, g_ref, wq_ref, keys_ref, idx_ref, gate_ref, st_ref):
    _peer_score_kernel(x_ref, g_ref, wq_ref, keys_ref, st_ref)
    _peer_topk_kernel(st_ref, idx_ref, gate_ref)


def _peer_select(x, g, wq, keys):
    n = x.shape[0]
    tm = _pick(n, (256, 128))
    rows = 2 * PEER_HEADS * N_KEYS
    return pl.pallas_call(
        _peer_select_kernel,
        out_shape=(jax.ShapeDtypeStruct((PEER_SLOTS, n), jnp.int32),
                   jax.ShapeDtypeStruct((PEER_SLOTS, n), F32)),
        grid=(n // tm,),
        in_specs=[pl.BlockSpec((tm, D_MODEL), lambda i: (i, 0)),
                  pl.BlockSpec((1, D_MODEL), lambda i: (0, 0)),
                  pl.BlockSpec((D_MODEL, rows), lambda i: (0, 0)),
                  pl.BlockSpec((2 * PEER_HEADS, N_KEYS, PEER_DKEY), lambda i: (0, 0, 0))],
        out_specs=(pl.BlockSpec((PEER_SLOTS, tm), lambda i: (0, i)),
                   pl.BlockSpec((PEER_SLOTS, tm), lambda i: (0, i))),
        scratch_shapes=[pltpu.VMEM((rows, tm), F32)],
        compiler_params=_cparams(("parallel",)),
        name="peer_select",
    )(x, g.reshape(1, D_MODEL), wq, keys)


LANES = 128
XROWS = D_MODEL // LANES
SLAB_PITCH = XROWS + 1
PEER_TB = 16
HI_MASK = -65536


def _pack_kernel(u_ref, v_ref, o_ref):
    for j in range(XROWS):
        sl = slice(j * LANES, (j + 1) * LANES)
        ub = lax.bitcast_convert_type(u_ref[:, sl].astype(BF16).astype(F32), jnp.int32)
        vb = lax.bitcast_convert_type(v_ref[:, sl].astype(BF16).astype(F32), jnp.int32)
        o_ref[:, j, :] = jnp.bitwise_or(ub, lax.shift_right_logical(vb, 16))


def _pack_uv(u_all, v_all, layer):
    e = u_all.shape[1]
    te = _pick(e, (256, 128))
    spec = pl.BlockSpec((None, te, D_MODEL), lambda i: (layer, i, 0))
    return pl.pallas_call(
        _pack_kernel,
        out_shape=jax.ShapeDtypeStruct((e, XROWS, LANES), jnp.int32),
        grid=(e // te,),
        in_specs=[spec, spec],
        out_specs=pl.BlockSpec((te, XROWS, LANES), lambda i: (i, 0, 0)),
        compiler_params=_cparams(("parallel",)),
        name="peer_pack",
    )(u_all, v_all)


def _peer_gather_kernel(idx0_ref, idx1_ref, idx2_ref, gate_ref, x_ref, g_ref, fg_ref, uv_ref, o_ref,
                        buf0_ref, buf1_ref, sem_ref, *, tb, final_norm):
    i = pl.program_id(0)
    last = pl.num_programs(0) - 1
    bufs = (buf0_ref, buf1_ref)

    def start_token(idx_ref, t, b):
        for k in range(PEER_SLOTS):
            r = t * PEER_SLOTS + k
            pltpu.async_copy(uv_ref.at[idx_ref[t, k]],
                             bufs[b].at[pl.ds(r * SLAB_PITCH, XROWS), :],
                             sem_ref.at[b], priority=k % 2)

    def wait_buf(b):
        filled = bufs[b].at[pl.ds(0, tb * PEER_SLOTS * XROWS), :]
        pltpu.make_async_copy(filled, filled, sem_ref.at[b]).wait()

    @pl.when(i == 0)
    def _():
        for t in range(tb):
            start_token(idx0_ref, t, 0)

    x = x_ref[...]
    xn = _rms(x, g_ref[...])
    lane = lax.broadcasted_iota(jnp.int32, (PEER_SLOTS, 2 * tb), 1)
    gates = gate_ref[0]
    outs = []
    for half, next_idx_ref in enumerate((idx1_ref, idx2_ref)):
        wait_buf(half)
        for t in range(tb):
            start_token(next_idx_ref, t, 1 - half)
            row = half * tb + t
            base = t * PEER_SLOTS * SLAB_PITCH

            def tile(j):
                return bufs[half][pl.ds(base + j, PEER_SLOTS, stride=SLAB_PITCH), :]

            acc = None
            for j in range(XROWS):
                u = lax.bitcast_convert_type(jnp.bitwise_and(tile(j), HI_MASK), F32)
                p = u * xn[row:row + 1, j * LANES:(j + 1) * LANES]
                acc = p if acc is None else acc + p
            hpre = jnp.sum(acc, axis=-1, keepdims=True)
            act = 0.5 * hpre * (1.0 + lax.erf(hpre * (2.0 ** -0.5)))
            gate_t = jnp.sum(jnp.where(lane == row, gates, 0.0), axis=-1, keepdims=True)
            w = gate_t * act
            rows = []
            for j in range(XROWS):
                v = lax.bitcast_convert_type(jnp.left_shift(tile(j), 16), F32)
                rows.append(jnp.sum(v * w, axis=0, keepdims=True))
            outs.append(jnp.concatenate(rows, axis=1))
    y = x + jnp.concatenate(outs, axis=0)
    if final_norm:
        y = _rms(y, fg_ref[...])
    o_ref[...] = y

    @pl.when(i == last)
    def _():
        wait_buf(0)


def _peer_gather(x, g, idx, gate, uv, final_g, final_norm, row0, nrows):
    tb = gate.shape[2] // 2
    nblk = nrows // tb
    b0 = row0 // tb
    assert nrows % (2 * tb) == 0 and row0 % (2 * tb) == 0
    kern = functools.partial(_peer_gather_kernel, tb=tb, final_norm=final_norm)
    gspec = pl.BlockSpec((1, D_MODEL), lambda i: (0, 0))
    ispec = lambda f: pl.BlockSpec((tb, PEER_SLOTS), lambda i: (b0 + f(i), 0), memory_space=pltpu.SMEM)
    buf = pltpu.VMEM((tb * PEER_SLOTS * SLAB_PITCH, LANES), jnp.int32)
    return pl.pallas_call(
        kern,
        out_shape=jax.ShapeDtypeStruct((nrows, D_MODEL), F32),
        grid=(nblk // 2,),
        in_specs=[ispec(lambda i: 0), ispec(lambda i: 2 * i + 1),
                  ispec(lambda i: jnp.minimum(2 * i + 2, nblk - 1)),
                  pl.BlockSpec((1, PEER_SLOTS, 2 * tb), lambda i: (b0 // 2 + i, 0, 0)),
                  pl.BlockSpec((2 * tb, D_MODEL), lambda i: (b0 // 2 + i, 0)),
                  gspec, gspec,
                  pl.BlockSpec(memory_space=pl.ANY)],
        out_specs=pl.BlockSpec((2 * tb, D_MODEL), lambda i: (i, 0)),
        scratch_shapes=[buf, buf, pltpu.SemaphoreType.DMA((2,))],
        compiler_params=_cparams(("arbitrary",), vmem=56 * 1024 * 1024),
        name="peer_gather",
    )(idx, idx, idx, gate, x, g.reshape(1, D_MODEL), final_g.reshape(1, D_MODEL), uv)


def _peer(x, g, wq, keys, uv, final_g, final_norm, splits):
    n = x.shape[0]
    idx_t, gate_t = _peer_select(x, g, wq, keys)
    idx = idx_t.T
    gate = gate_t.reshape(PEER_SLOTS, n // (2 * PEER_TB), 2 * PEER_TB).transpose(1, 0, 2)
    return [_peer_gather(x, g, idx, gate, uv, final_g, final_norm, r0, nr) for r0, nr in splits]


def kernel(x_prompt, x_sample, state_conv, state_lru, state_hgrn, cache_k, cache_v, norm1_g, w_in,
           conv_w, conv_b, lru_wa, lru_ba, lru_wx, lru_bx, lru_lambda, hg_lb, hg_norm_g, attn_sink,
           w_branch, w_out, norm2_g, peer_wq, peer_keys, peer_u, peer_v, final_g):
    depth = w_in.shape[0]
    bp, tp, _ = x_prompt.shape
    bs, ts, _ = x_sample.shape
    tpad = SAMPLE_PAD_T
    assert ts <= tpad and tp % WINDOW == 0 and ts >= CONV_W - 1 and cache_k.shape[2] == WINDOW
    n_p = bp * tp
    n_s = bs * tpad

    xs_pad = jnp.pad(x_sample, ((0, 0), (0, tpad - ts), (0, 0)))
    x = jnp.concatenate([x_prompt.reshape(n_p, D_MODEL), xs_pad.reshape(n_s, D_MODEL)], axis=0)

    zeros_conv = jnp.zeros((bp, SUBLANES, MIX_W), F32)
    zeros_h = jnp.zeros((bp, MIX_W), F32)
    zeros_s = jnp.zeros((1, bp) + state_hgrn.shape[2:], F32)

    p_st, s_st = [], []
    for l in range(depth):
        lw = dict(conv_w=conv_w[l], conv_b=conv_b[l].reshape(1, MIX_W),
                  wa=lru_wa[l].astype(BF16), ba=lru_ba[l].reshape(1, MIX_W),
                  wx=lru_wx[l].astype(BF16), bx=lru_bx[l].reshape(1, MIX_W),
                  lam=lru_lambda[l].reshape(1, MIX_W))
        proj = _norm_matmul(x, norm1_g[l], _layer_bf16(w_in, l))

        ya_p, h_p, xl_p = _conv_lru(proj, 0, bp, tp, tp, zeros_conv, zeros_h, lw)
        yb_p, st_p = _hgrn(proj, 0, bp, tp, tp, zeros_s, 0, hg_lb, hg_norm_g[l], l)
        yc_p, kn_p, vn_p = _swa(proj, 0, bp, tp, attn_sink[l])
        conv0 = jnp.pad(state_conv[l], ((0, 0), (SUBLANES - (CONV_W - 1), 0), (0, 0)))
        ya_s, h_s, xl_s = _conv_lru(proj, n_p, bs, tpad, ts, conv0, state_lru[l], lw)
        yb_s, st_s = _hgrn(proj, n_p, bs, tpad, ts, state_hgrn, l, hg_lb, hg_norm_g[l], l)
        yc_s, kn_s, vn_s = _swa(
            proj, n_p, bs, tpad, attn_sink[l],
            cache=(cache_k[l].reshape(bs * WINDOW, KV_W), cache_v[l].reshape(bs * WINDOW, KV_W)))

        z = _branch_mix((ya_p, yb_p, yc_p), (ya_s, yb_s, yc_s), proj, w_branch[l].astype(BF16))
        x = _out_proj(z, w_out[l].astype(BF16), x)

        uv = _pack_uv(peer_u, peer_v, l)
        keys = peer_keys[l].reshape(2 * PEER_HEADS, N_KEYS, PEER_DKEY).astype(BF16)
        last = l == depth - 1
        splits = [(0, n_p), (n_p, n_s)] if last else [(0, n_p + n_s)]
        outs = _peer(x, norm2_g[l], peer_wq[l].astype(BF16), keys, uv, final_g, last, splits)
        x = outs[0]

        p_st.append((xl_p[:, SUBLANES - (CONV_W - 1):], h_p, st_p,
                     kn_p.reshape(bp, WINDOW, KV_HEADS, HEAD_DIM),
                     vn_p.reshape(bp, WINDOW, KV_HEADS, HEAD_DIM)))
        k_new = kn_s[:, :ts].reshape(bs, ts, KV_HEADS, HEAD_DIM)
        v_new = vn_s[:, :ts].reshape(bs, ts, KV_HEADS, HEAD_DIM)
        s_st.append((xl_s[:, ts - (CONV_W - 1):ts], h_s, st_s,
                     jnp.concatenate([cache_k[l][:, ts:], k_new], axis=1),
                     jnp.concatenate([cache_v[l][:, ts:], v_new], axis=1)))

    y_prompt = outs[0].reshape(bp, tp, D_MODEL)
    y_sample = outs[1].reshape(bs, tpad, D_MODEL)[:, :ts]
    stack = lambda sts, i: jnp.stack([s[i] for s in sts])
    return (y_prompt, y_sample,
            stack(p_st, 0), stack(p_st, 1), stack(p_st, 2), stack(p_st, 3), stack(p_st, 4),
            stack(s_st, 0), stack(s_st, 1), stack(s_st, 2), stack(s_st, 3), stack(s_st, 4))
```
